```python
import jax
import jax.numpy as jnp
from jax import lax
import numpy as np

D_MODEL = 2048
BATCH = 4
SEQ = 4096
DEPTH = 4

GRID_W = 64
CTX_LEN = 256
N_MIXERS = 4
FFN_HIDDEN = ((8 * D_MODEL + 3 * 256 - 1) // (3 * 256)) * 256
RET_DK = 256
RET_DV = 512
RET_HEADS = D_MODEL // RET_DK
RET_CHUNK = 128
GQA_HEAD_DIM = 128
GQA_HEADS = D_MODEL // GQA_HEAD_DIM
GQA_KV_HEADS = GQA_HEADS // 4
MLA_HEADS = D_MODEL // 128
MLA_Q_RANK = 512
MLA_KV_RANK = 512
MLA_NOPE = 128
MLA_ROPE = 64
MLA_V = 128
HGRN_DIM = 128
HGRN_HEADS = D_MODEL // HGRN_DIM
HGRN_CHUNK = 64

ATTN_BLOCK = 128
ROPE_THETA = 10000.0
EPS = 1e-6
DEEPNORM_ALPHA = (2 * DEPTH) ** 0.25
DEEPNORM_BETA = (8 * DEPTH) ** -0.25
F32 = jnp.float32

kernel_name = "hybrid_retention_gqa_mla_hgrn2_diffusion_trunk"


def layer_norm(x, g, b):
    xf = x.astype(F32)
    xc = xf - jnp.mean(xf, -1, keepdims=True)
    var = jnp.mean(xc * xc, -1, keepdims=True)
    return (xc * lax.rsqrt(var + EPS) * g.astype(F32) + b.astype(F32)).astype(x.dtype)


def rms_norm(x, g=None):
    xf = x.astype(F32)
    y = xf * lax.rsqrt(jnp.mean(xf * xf, -1, keepdims=True) + EPS)
    if g is not None:
        y = y * g.astype(F32)
    return y.astype(x.dtype)


def rope_1d(x, ang):
    cos = jnp.cos(ang)[:, None, :]
    sin = jnp.sin(ang)[:, None, :]
    x1, x2 = jnp.split(x.astype(F32), 2, axis=-1)
    return jnp.concatenate([x1 * cos - x2 * sin, x1 * sin + x2 * cos], -1).astype(x.dtype)


def rope_2d(x, row, col):
    half = x.shape[-1] // 2
    freqs = ROPE_THETA ** (-jnp.arange(0, half, 2, dtype=F32) / half)
    a_row = row.astype(F32)[:, None] * freqs
    a_col = col.astype(F32)[:, None] * freqs
    return jnp.concatenate([rope_1d(x[..., :half], a_row), rope_1d(x[..., half:], a_col)], -1)


def modulate(z, shift, scale):
    return z * (1.0 + scale) + shift


def swiglu(z, w_in, w_out):
    a, b = jnp.split(z @ w_in, 2, axis=-1)
    return (jax.nn.silu(a) * b) @ w_out


def attention(q, k, v, scale):
    s = jnp.einsum('bqhgd,bkhd->bhgqk', q, k).astype(F32) * scale
    p = jax.nn.softmax(s, axis=-1).astype(v.dtype)
    return jnp.einsum('bhgqk,bkhv->bqhgv', p, v)


def blocked_attention(q, k, v, scale):
    B, L = q.shape[:2]
    qb = jnp.moveaxis(q.reshape((B, L // ATTN_BLOCK, ATTN_BLOCK) + q.shape[2:]), 1, 0)
    out = lax.map(lambda blk: attention(blk, k, v, scale), qb)
    out = jnp.moveaxis(out, 0, 1)
    return out.reshape((B, L) + out.shape[3:])


def flip_t(t):
    return jnp.flip(t, axis=2)


def retention_scan(q, k, v, log_gamma, state):
    B, H, L, _ = q.shape
    dv = v.shape[-1]
    C = RET_CHUNK
    lg = log_gamma.astype(F32)
    pos = jnp.arange(C, dtype=F32)
    diff = pos[:, None] - pos[None, :]
    decay_mask = jnp.where(diff >= 0, jnp.exp(jnp.maximum(diff, 0.0) * lg[:, None, None]), 0.0)
    q_decay = jnp.exp((pos + 1.0) * lg[:, None])[..., None]
    k_decay = jnp.exp((C - 1.0 - pos) * lg[:, None])[..., None]
    chunk_decay = jnp.exp(C * lg)[:, None, None]

    def to_chunks(t):
        return jnp.moveaxis(t.astype(F32).reshape(B, H, L // C, C, t.shape[-1]), 2, 0)

    def step(S, xs):
        qc, kc, vc = xs
        scores = jnp.einsum('bhcd,bhsd->bhcs', qc, kc) * decay_mask
        o = jnp.einsum('bhcs,bhsv->bhcv', scores, vc) + jnp.einsum('bhcd,bhdv->bhcv', qc * q_decay, S)
        S = S * chunk_decay + jnp.einsum('bhsd,bhsv->bhdv', kc * k_decay, vc)
        return S, o

    S, o = lax.scan(step, state, (to_chunks(q), to_chunks(k), to_chunks(v)))
    return jnp.moveaxis(o, 0, 2).reshape(B, H, L, dv), S


def gla_scan(q, k, v, log_f, state):
    B, H, L, _ = q.shape
    dv = v.shape[-1]
    C = HGRN_CHUNK
    tri = jnp.arange(C)[:, None] >= jnp.arange(C)[None, :]

    def to_chunks(t):
        return jnp.moveaxis(t.reshape(B, H, L // C, C, t.shape[-1]), 2, 0)

    def step(S, xs):
        qc, kc, vc, gc = xs
        b = jnp.cumsum(gc, axis=-2)
        b_last = b[..., -1:, :]
        q_in = qc * jnp.exp(b)
        scores = jnp.where(tri, jnp.einsum('bhcd,bhsd->bhcs', q_in, kc * jnp.exp(-b)), 0.0)
        o = jnp.einsum('bhcd,bhdv->bhcv', q_in, S) + jnp.einsum('bhcs,bhsv->bhcv', scores, vc)
        S = S * jnp.exp(b_last)[..., 0, :, None] + jnp.einsum('bhsd,bhsv->bhdv', kc * jnp.exp(b_last - b), vc)
        return S, o

    S, o = lax.scan(step, state, (to_chunks(q), to_chunks(k), to_chunks(v), to_chunks(log_f)))
    return jnp.moveaxis(o, 0, 2).reshape(B, H, L, dv), S


def retention_mixer(h, hc, w_in, lg_fwd, lg_bwd, w_out, row, col, need_ctx):
    B = h.shape[0]
    hk, hv = RET_HEADS * RET_DK, RET_HEADS * RET_DV

    def project(z, rotate):
        L = z.shape[1]
        q, k, v, g = jnp.split(z @ w_in, [hk, 2 * hk, 2 * hk + hv], axis=-1)
        q = q.reshape(B, L, RET_HEADS, RET_DK)
        k = k.reshape(B, L, RET_HEADS, RET_DK)
        v = v.reshape(B, L, RET_HEADS, RET_DV)
        if rotate:
            q = rope_2d(q, row, col)
            k = rope_2d(k, row, col)
        heads = lambda t: jnp.swapaxes(t, 1, 2).astype(F32)
        return heads(q), heads(k) * RET_DK ** -0.5, heads(v), g

    def bidir(q, k, v, s_f, s_b):
        o_f, s_f = retention_scan(q, k, v, lg_fwd, s_f)
        o_b, s_b = retention_scan(flip_t(q), flip_t(k), flip_t(v), lg_bwd, s_b)
        return o_f + flip_t(o_b), s_f, s_b

    def readout(o, g):
        o = rms_norm(jnp.swapaxes(o, 1, 2))
        o = o.reshape(B, o.shape[1], hv).astype(g.dtype) * jax.nn.silu(g)
        return o @ w_out

    qc, kc, vc, gc = project(hc, False)
    q, k, v, g = project(h, True)
    zero = jnp.zeros((B, RET_HEADS, RET_DK, RET_DV), F32)
    oc, s_f, s_b = bidir(qc, kc, vc, zero, zero)
    o, _, _ = bidir(q, k, v, s_f, s_b)
    y = readout(o, g)
    yc = readout(oc, gc) if need_ctx else None
    return y, yc


def gqa_mixer(h, hc, w_in, q_gain, k_gain, w_out, row, col, need_ctx):
    B = h.shape[0]
    d = GQA_HEAD_DIM
    G = GQA_HEADS // GQA_KV_HEADS
    scale = d ** -0.5

    def project(z, rotate):
        L = z.shape[1]
        q, k, v = jnp.split(z @ w_in, [GQA_HEADS * d, (GQA_HEADS + GQA_KV_HEADS) * d], axis=-1)
        q = rms_norm(q.reshape(B, L, GQA_HEADS, d), q_gain)
        k = rms_norm(k.reshape(B, L, GQA_KV_HEADS, d), k_gain)
        v = v.reshape(B, L, GQA_KV_HEADS, d)
        if rotate:
            q = rope_2d(q, row, col)
            k = rope_2d(k, row, col)
        return q.reshape(B, L, GQA_KV_HEADS, G, d), k, v

    qc, kc, vc = project(hc, False)
    q, k, v = project(h, True)
    k_all = jnp.concatenate([kc, k], axis=1)
    v_all = jnp.concatenate([vc, v], axis=1)
    o = blocked_attention(q, k_all, v_all, scale)
    y = o.reshape(B, o.shape[1], GQA_HEADS * d) @ w_out
    yc = None
    if need_ctx:
        oc = attention(qc, kc, vc, scale)
        yc = oc.reshape(B, oc.shape[1], GQA_HEADS * d) @ w_out
    return y, yc


def mla_mixer(h, hc, w_in, q_norm, w_q_up, kv_norm, w_kv_up, w_out, row, col, need_ctx):
    B = h.shape[0]
    H = MLA_HEADS
    scale = (MLA_NOPE + MLA_ROPE) ** -0.5

    def project(z, rotate):
        L = z.shape[1]
        cq, ckv, k_rope = jnp.split(z @ w_in, [MLA_Q_RANK, MLA_Q_RANK + MLA_KV_RANK], axis=-1)
        q = (rms_norm(cq, q_norm) @ w_q_up).reshape(B, L, H, MLA_NOPE + MLA_ROPE)
        kv = (rms_norm(ckv, kv_norm) @ w_kv_up).reshape(B, L, H, MLA_NOPE + MLA_V)
        q_nope, q_rope = jnp.split(q, [MLA_NOPE], axis=-1)
        k_nope, v = jnp.split(kv, [MLA_NOPE], axis=-1)
        k_rope = k_rope[:, :, None, :]
        if rotate:
            q_rope = rope_2d(q_rope, row, col)
            k_rope = rope_2d(k_rope, row, col)
        q = jnp.concatenate([q_nope, q_rope], -1)[:, :, :, None, :]
        k = jnp.concatenate([k_nope, jnp.broadcast_to(k_rope, (B, L, H, MLA_ROPE))], -1)
        return q, k, v

    qc, kc, vc = project(hc, False)
    q, k, v = project(h, True)
    k_all = jnp.concatenate([kc, k], axis=1)
    v_all = jnp.concatenate([vc, v], axis=1)
    o = blocked_attention(q, k_all, v_all, scale)
    y = o.reshape(B, o.shape[1], H * MLA_V) @ w_out
    yc = None
    if need_ctx:
        oc = attention(qc, kc, vc, scale)
        yc = oc.reshape(B, oc.shape[1], H * MLA_V) @ w_out
    return y, yc


def hgrn2_mixer(h, hc, w_in, lower_bound, out_gain, w_out, need_ctx):
    B = h.shape[0]
    H, d = HGRN_HEADS, HGRN_DIM
    width = H * d
    lb = lower_bound.astype(F32).reshape(H, 1, d)

    def heads(t):
        return jnp.swapaxes(t.reshape(B, t.shape[1], H, d), 1, 2).astype(F32)

    def forget(fp):
        f = lb + (1.0 - lb) * jax.nn.sigmoid(heads(fp))
        return 1.0 - f, jnp.log(f)

    def project(z):
        q, f_f, f_b, i, g = jnp.split(z @ w_in, 5, axis=-1)
        k_f, lf_f = forget(f_f)
        k_b, lf_b = forget(f_b)
        return jax.nn.silu(heads(q)), heads(i), k_f, lf_f, k_b, lf_b, g

    def bidir(q, i, k_f, lf_f, k_b, lf_b, s_f, s_b):
        o_f, s_f = gla_scan(q, k_f, i, lf_f, s_f)
        o_b, s_b = gla_scan(flip_t(q), flip_t(k_b), flip_t(i), flip_t(lf_b), s_b)
        return o_f + flip_t(o_b), s_f, s_b

    def readout(o, g):
        o = rms_norm(jnp.swapaxes(o, 1, 2), out_gain).reshape(B, o.shape[2], width)
        return (o.astype(g.dtype) * jax.nn.silu(g)) @ w_out

    qc, ic, kfc, lfc, kbc, lbc, gc = project(hc)
    q, i, k_f, lf_f, k_b, lf_b, g = project(h)
    zero = jnp.zeros((B, H, d, d), F32)
    oc, s_f, s_b = bidir(qc, ic, kfc, lfc, kbc, lbc, zero, zero)
    o, _, _ = bidir(q, i, k_f, lf_f, k_b, lf_b, s_f, s_b)
    y = readout(o, g)
    yc = readout(oc, gc) if need_ctx else None
    return y, yc


def setup_inputs(seed: int = 0) -> dict:
    key = jax.random.key(seed)
    ks = jax.random.split(key, 32)
    counter = iter(range(32))
    D = D_MODEL

    def normal(shape, scale):
        return jax.random.normal(ks[next(counter)], shape, F32) * scale

    def n_of(m):
        return len(range(m, DEPTH, N_MIXERS))

    nR, nG, nM, nH = n_of(0), n_of(1), n_of(2), n_of(3)
    ret_base = jnp.log(1.0 - 2.0 ** (-5.0 - jnp.arange(RET_HEADS, dtype=F32)))
    ret_in = 2 * RET_HEADS * RET_DK + 2 * RET_HEADS * RET_DV
    gqa_in = (GQA_HEADS + 2 * GQA_KV_HEADS) * GQA_HEAD_DIM
    mla_in = MLA_Q_RANK + MLA_KV_RANK + MLA_ROPE
    hgrn_w = HGRN_HEADS * HGRN_DIM
    return {
        "x": normal((BATCH, SEQ, D), 1.0),
        "c": normal((BATCH, D), 1.0),
        "ctx": normal((BATCH, CTX_LEN, D), 1.0),
        "c_ctx": normal((D,), 1.0),
        "ada_w": normal((DEPTH, D, 6 * D), 0.5 * D ** -0.5),
        "ada_b": normal((DEPTH, 6 * D), 0.01),
        "ln_g": 1.0 + normal((DEPTH, 2, D), 0.02),
        "ln_b": normal((DEPTH, 2, D), 0.01),
        "ffn_w_in": normal((DEPTH, D, 2 * FFN_HIDDEN), D ** -0.5),
        "ffn_w_out": normal((DEPTH, FFN_HIDDEN, D), DEEPNORM_BETA * FFN_HIDDEN ** -0.5),
        "ret_w_in": normal((nR, D, ret_in), D ** -0.5),
        "ret_decay_fwd": ret_base[None] * (1.0 + normal((nR, RET_HEADS), 0.05)),
        "ret_decay_bwd": ret_base[None] * (1.0 + normal((nR, RET_HEADS), 0.05)),
        "ret_w_out": normal((nR, RET_HEADS * RET_DV, D), DEEPNORM_BETA * (RET_HEADS * RET_DV) ** -0.5),
        "gqa_w_in": normal((nG, D, gqa_in), D ** -0.5),
        "gqa_q_norm": 1.0 + normal((nG, GQA_HEAD_DIM), 0.02),
        "gqa_k_norm": 1.0 + normal((nG, GQA_HEAD_DIM), 0.02),
        "gqa_w_out": normal((nG, GQA_HEADS * GQA_HEAD_DIM, D), DEEPNORM_BETA * (GQA_HEADS * GQA_HEAD_DIM) ** -0.5),
        "mla_w_in": normal((nM, D, mla_in), D ** -0.5),
        "mla_q_norm": 1.0 + normal((nM, MLA_Q_RANK), 0.02),
        "mla_w_q_up": normal((nM, MLA_Q_RANK, MLA_HEADS * (MLA_NOPE + MLA_ROPE)), MLA_Q_RANK ** -0.5),
        "mla_kv_norm": 1.0 + normal((nM, MLA_KV_RANK), 0.02),
        "mla_w_kv_up": normal((nM, MLA_KV_RANK, MLA_HEADS * (MLA_NOPE + MLA_V)), MLA_KV_RANK ** -0.5),
        "mla_w_out": normal((nM, MLA_HEADS * MLA_V, D), DEEPNORM_BETA * (MLA_HEADS * MLA_V) ** -0.5),
        "hgrn_w_in": normal((nH, D, 5 * hgrn_w), D ** -0.5),
        "hgrn_lb_raw": normal((DEPTH, hgrn_w), 0.1),
        "hgrn_out_norm": 1.0 + normal((nH, HGRN_DIM), 0.02),
        "hgrn_w_out": normal((nH, hgrn_w, D), DEEPNORM_BETA * hgrn_w ** -0.5),
    }


def reference(x, c, ctx, c_ctx, ada_w, ada_b, ln_g, ln_b, ffn_w_in, ffn_w_out,
              ret_w_in, ret_decay_fwd, ret_decay_bwd, ret_w_out,
              gqa_w_in, gqa_q_norm, gqa_k_norm, gqa_w_out,
              mla_w_in, mla_q_norm, mla_w_q_up, mla_kv_norm, mla_w_kv_up, mla_w_out,
              hgrn_w_in, hgrn_lb_raw, hgrn_out_norm, hgrn_w_out):
    rows = x.shape[1] // GRID_W
    row = jnp.repeat(jnp.arange(rows), GRID_W)
    col = jnp.tile(jnp.arange(GRID_W), rows)
    p = jax.nn.softmax(hgrn_lb_raw.astype(F32), axis=0)
    lower_bounds = jnp.cumsum(p, axis=0) - p[0]

    for i in range(DEPTH):
        m, j = i % N_MIXERS, i // N_MIXERS
        need_ctx = i < DEPTH - 1
        sh1, sc1, g1, sh2, sc2, g2 = jnp.split(jax.nn.silu(c) @ ada_w[i] + ada_b[i], 6, axis=-1)
        csh1, csc1, cg1, csh2, csc2, cg2 = jnp.split(jax.nn.silu(c_ctx) @ ada_w[i] + ada_b[i], 6, axis=-1)
        h = modulate(x, sh1[:, None], sc1[:, None])
        hc = modulate(ctx, csh1, csc1)
        if m == 0:
            y, yc = retention_mixer(h, hc, ret_w_in[j], ret_decay_fwd[j], ret_decay_bwd[j], ret_w_out[j],
                                    row, col, need_ctx)
        elif m == 1:
            y, yc = gqa_mixer(h, hc, gqa_w_in[j], gqa_q_norm[j], gqa_k_norm[j], gqa_w_out[j],
                              row, col, need_ctx)
        elif m == 2:
            y, yc = mla_mixer(h, hc, mla_w_in[j], mla_q_norm[j], mla_w_q_up[j], mla_kv_norm[j],
                              mla_w_kv_up[j], mla_w_out[j], row, col, need_ctx)
        else:
            y, yc = hgrn2_mixer(h, hc, hgrn_w_in[j], lower_bounds[i], hgrn_out_norm[j], hgrn_w_out[j], need_ctx)
        x = layer_norm(DEEPNORM_ALPHA * x + g1[:, None] * y, ln_g[i, 0], ln_b[i, 0])
        h = modulate(x, sh2[:, None], sc2[:, None])
        x = layer_norm(DEEPNORM_ALPHA * x + g2[:, None] * swiglu(h, ffn_w_in[i], ffn_w_out[i]), ln_g[i, 1], ln_b[i, 1])
        if need_ctx:
            ctx = layer_norm(DEEPNORM_ALPHA * ctx + cg1 * yc, ln_g[i, 0], ln_b[i, 0])
            hc = modulate(ctx, csh2, csc2)
            ctx = layer_norm(DEEPNORM_ALPHA * ctx + cg2 * swiglu(hc, ffn_w_in[i], ffn_w_out[i]), ln_g[i, 1], ln_b[i, 1])
    return x
```

```python
import functools

import jax
import jax.numpy as jnp
from jax import lax
from jax.experimental import pallas as pl
from jax.experimental.pallas import tpu as pltpu

F32 = jnp.float32
BF16 = jnp.bfloat16

GRID_W = 64
N_MIXERS = 4
RET_DK = 256
RET_DV = 512
RET_CHUNK = 128
GQA_HEAD_DIM = 128
GQA_GROUP = 4
MLA_NOPE = 128
MLA_ROPE = 64
MLA_V = 128
HGRN_DIM = 128
HGRN_CHUNK = 64
ROPE_THETA = 10000.0
EPS = 1e-6

LANES = 128
ROW_TILE = 256
ATTN_KV_CHUNK = 256
V7X_VMEM_LIMIT = 56 * 1024 * 1024


def _params(*sem):
    return pltpu.CompilerParams(dimension_semantics=sem, vmem_limit_bytes=V7X_VMEM_LIMIT)


def _dot(a, b):
    return jnp.dot(a, b, preferred_element_type=F32)


def _dot_nt(a, b):
    return lax.dot_general(a, b, (((1,), (1,)), ((), ())), preferred_element_type=F32)


def _dot_tn(a, b):
    return lax.dot_general(a, b, (((0,), (0,)), ((), ())), preferred_element_type=F32)


def _silu(x):
    return x * jax.nn.sigmoid(x)


def _swap_halves(x, w):
    if 2 * w == LANES:
        return pltpu.roll(x, w, axis=1)
    up = pltpu.roll(x, LANES - w, axis=1)
    down = pltpu.roll(x, w, axis=1)
    lane = lax.broadcasted_iota(jnp.int32, x.shape, 1)
    return jnp.where((lane % (2 * w)) < w, up, down)


def _ada_kernel(c_ref, w_ref, b_ref, o_ref):
    s = _silu(c_ref[...]).astype(BF16)
    o_ref[...] = _dot(s, w_ref[...].astype(BF16)) + b_ref[...]


def _ada_all(cc, ada_w, ada_b, tn=1024):
    depth, d, n = ada_w.shape
    rows = cc.shape[0]
    return pl.pallas_call(
        _ada_kernel,
        grid=(depth, n // tn),
        in_specs=[pl.BlockSpec((rows, d), lambda l, j: (0, 0)),
                  pl.BlockSpec((None, d, tn), lambda l, j: (l, 0, j)),
                  pl.BlockSpec((None, 1, tn), lambda l, j: (l, 0, j))],
        out_specs=pl.BlockSpec((None, rows, tn), lambda l, j: (l, 0, j)),
        out_shape=jax.ShapeDtypeStruct((depth, rows, n), F32),
        compiler_params=_params("arbitrary", "arbitrary"),
        name="ada_mod",
    )(cc, ada_w, ada_b.reshape(depth, 1, n))


def _mod_kernel(x_ref, sh_ref, sc_ref, h_ref):
    h_ref[...] = (x_ref[...] * (1.0 + sc_ref[...]) + sh_ref[...]).astype(BF16)


def _modulate0(x, sh, sc, n_lat_tiles):
    b, t, d = x.shape
    vec = pl.BlockSpec((None, None, 1, d), lambda i, j: (i, j // n_lat_tiles, 0, 0))
    return pl.pallas_call(
        _mod_kernel,
        grid=(b, t // ROW_TILE),
        in_specs=[pl.BlockSpec((None, ROW_TILE, d), lambda i, j: (i, j, 0)), vec, vec],
        out_specs=pl.BlockSpec((None, ROW_TILE, d), lambda i, j: (i, j, 0)),
        out_shape=jax.ShapeDtypeStruct((b, t, d), BF16),
        compiler_params=_params("arbitrary", "arbitrary"),
        name="modulate0",
    )(x, sh, sc)


def _mm_call(body, h, w, *, col_blk0, n_blk, tn, tm, out_dtypes, name, row_ins=(), col_ins=()):
    rows, k = h.shape
    assert rows % tm == 0 and w.shape[0] == k
    in_specs = [pl.BlockSpec((tm, k), lambda j, i: (i, 0)),
                pl.BlockSpec((k, tn), lambda j, i: (0, j + col_blk0))]
    args = [h, w]
    for a in row_ins:
        in_specs.append(pl.BlockSpec((tm, a.shape[1]), lambda j, i: (i, 0)))
        args.append(a)
    for a in col_ins:
        assert a.shape[1] == n_blk * tn
        in_specs.append(pl.BlockSpec((a.shape[0], tn), lambda j, i: (0, j)))
        args.append(a)
    out = pl.pallas_call(
        body,
        grid=(n_blk, rows // tm),
        in_specs=in_specs,
        out_specs=[pl.BlockSpec((tm, tn), lambda j, i: (i, j)) for _ in out_dtypes],
        out_shape=[jax.ShapeDtypeStruct((rows, n_blk * tn), dt) for dt in out_dtypes],
        compiler_params=_params("arbitrary", "arbitrary"),
        name=name,
    )(*args)
    return out


def _plain_body(h_ref, w_ref, o_ref, *, scale):
    acc = _dot(h_ref[...], w_ref[...])
    if scale != 1.0:
        acc = acc * scale
    o_ref[...] = acc.astype(o_ref.dtype)


def _silu_body(h_ref, w_ref, o_ref):
    o_ref[...] = _silu(_dot(h_ref[...], w_ref[...])).astype(o_ref.dtype)


def _rope_body(h_ref, w_ref, cos_ref, sin_ref, gain_ref, o_ref, *, swap, norm):
    acc = _dot(h_ref[...], w_ref[...])
    tn = acc.shape[1]
    tw = cos_ref.shape[1]
    for c in range(tn // LANES):
        sl = slice(c * LANES, (c + 1) * LANES)
        x = acc[:, sl]
        if norm:
            x = x * lax.rsqrt(jnp.mean(x * x, axis=-1, keepdims=True) + EPS)
        x = x * gain_ref[:, sl]
        t = c % (tw // LANES)
        tsl = slice(t * LANES, (t + 1) * LANES)
        y = x * cos_ref[:, tsl] + _swap_halves(x, swap) * sin_ref[:, tsl]
        o_ref[:, sl] = y.astype(o_ref.dtype)


def _forget_body(h_ref, w_ref, raw_ref, k_ref, lf_ref, *, layer):
    raw = raw_ref[...]
    e = jnp.exp(raw - jnp.max(raw, axis=0, keepdims=True))
    p = e / jnp.sum(e, axis=0, keepdims=True)
    lb = jnp.zeros_like(p[0:1])
    for r in range(1, layer + 1):
        lb = lb + p[r:r + 1]
    f = lb + (1.0 - lb) * jax.nn.sigmoid(_dot(h_ref[...], w_ref[...]))
    k_ref[...] = (1.0 - f).astype(k_ref.dtype)
    lf_ref[...] = jnp.log(f)


def _mla_in_body(h_ref, w_ref, cos_ref, sin_ref, qn_ref, kvn_ref, cq_ref, ckv_ref, kr_ref, *, qr, kvr):
    acc = _dot(h_ref[...], w_ref[...])
    cq = acc[:, :qr]
    cq_ref[...] = (cq * lax.rsqrt(jnp.mean(cq * cq, axis=-1, keepdims=True) + EPS) * qn_ref[...]).astype(BF16)
    ckv = acc[:, qr:qr + kvr]
    ckv_ref[...] = (ckv * lax.rsqrt(jnp.mean(ckv * ckv, axis=-1, keepdims=True) + EPS) * kvn_ref[...]).astype(BF16)
    kr = acc[:, qr + kvr:]
    kr_ref[...] = (kr * cos_ref[...] + _swap_halves(kr, MLA_ROPE // 4) * sin_ref[...]).astype(BF16)


def _swiglu_body(h_ref, wa_ref, wb_ref, o_ref):
    h = h_ref[...]
    a = _dot(h, wa_ref[...])
    b = _dot(h, wb_ref[...])
    o_ref[...] = (_silu(a) * b).astype(o_ref.dtype)


def _ffn_in(h, w, tm, tn):
    rows, k = h.shape
    hidden = w.shape[1] // 2
    nb = hidden // tn
    return pl.pallas_call(
        _swiglu_body,
        grid=(nb, rows // tm),
        in_specs=[pl.BlockSpec((tm, k), lambda j, i: (i, 0)),
                  pl.BlockSpec((k, tn), lambda j, i: (0, j)),
                  pl.BlockSpec((k, tn), lambda j, i: (0, j + nb))],
        out_specs=pl.BlockSpec((tm, tn), lambda j, i: (i, j)),
        out_shape=jax.ShapeDtypeStruct((rows, hidden), BF16),
        compiler_params=_params("arbitrary", "arbitrary"),
        name="ffn_in_swiglu",
    )(h, w, w)


def _outproj_ln_kernel(a_ref, w_ref, x_ref, gate_ref, lng_ref, lnb_ref, sh_ref, sc_ref, *out_refs,
                       alpha, emit_h):
    y = _dot(a_ref[...], w_ref[...])
    z = alpha * x_ref[...] + gate_ref[...] * y
    zc = z - jnp.mean(z, axis=-1, keepdims=True)
    var = jnp.mean(zc * zc, axis=-1, keepdims=True)
    xn = zc * lax.rsqrt(var + EPS) * lng_ref[...] + lnb_ref[...]
    out_refs[0][...] = xn
    if emit_h:
        out_refs[1][...] = (xn * (1.0 + sc_ref[...]) + sh_ref[...]).astype(BF16)


def _outproj_ln(a, w, x, gate, lng, lnb, sh, sc, *, alpha, n_lat_tiles, n_tiles, emit_h, name):
    b, _, k = a.shape
    d = w.shape[1]
    tile = lambda width: pl.BlockSpec((None, ROW_TILE, width), lambda i, j: (i, j, 0))
    vec = pl.BlockSpec((None, None, 1, d), lambda i, j: (i, j // n_lat_tiles, 0, 0))
    full = pl.BlockSpec((1, d), lambda i, j: (0, 0))
    out_shape = [jax.ShapeDtypeStruct((b, n_tiles * ROW_TILE, d), F32)]
    out_specs = [tile(d)]
    if emit_h:
        out_shape.append(jax.ShapeDtypeStruct((b, n_tiles * ROW_TILE, d), BF16))
        out_specs.append(tile(d))
    return pl.pallas_call(
        functools.partial(_outproj_ln_kernel, alpha=alpha, emit_h=emit_h),
        grid=(b, n_tiles),
        in_specs=[tile(k),
                  pl.BlockSpec((k, d), lambda i, j: (0, 0), pipeline_mode=pl.Buffered(1)),
                  tile(d), vec, full, full, vec, vec],
        out_specs=out_specs,
        out_shape=out_shape,
        compiler_params=_params("arbitrary", "arbitrary"),
        name=name,
    )(a, w, x, gate, lng, lnb, sh, sc)


def _attn_steps(qs, k_chunk, v_chunk, lo, hi, m_ref, l_ref, acc_ref):
    m_ref[...] = jnp.full(m_ref.shape, -jnp.inf, F32)
    l_ref[...] = jnp.zeros(l_ref.shape, F32)
    acc_ref[...] = jnp.zeros(acc_ref.shape, F32)

    def step(c, carry):
        s = _dot_nt(qs, k_chunk(c))
        m_old = m_ref[...]
        m_new = jnp.maximum(m_old, jnp.max(s, axis=-1, keepdims=True))
        alpha = jnp.exp(m_old - m_new)
        p = jnp.exp(s - m_new)
        l_ref[...] = alpha * l_ref[...] + jnp.sum(p, axis=-1, keepdims=True)
        acc_ref[...] = alpha * acc_ref[...] + _dot(p.astype(BF16), v_chunk(c))
        m_ref[...] = m_new
        return carry

    lax.fori_loop(lo, hi, step, 0)
    return acc_ref[...] / l_ref[...]


def _gqa_attn_kernel(q_ref, k_ref, v_ref, o_ref, m_ref, l_ref, acc_ref, *, n_lat_tiles, n_chunks, n_lat_chunks):
    d = GQA_HEAD_DIM
    tq = q_ref.shape[0]
    is_ctx = pl.program_id(2) >= n_lat_tiles
    lo = jnp.where(is_ctx, n_lat_chunks, 0)
    q = q_ref[...]
    qs = jnp.concatenate([q[:, g * d:(g + 1) * d] for g in range(GQA_GROUP)], axis=0)

    def k_chunk(c):
        return k_ref[pl.ds(pl.multiple_of(c * ATTN_KV_CHUNK, ATTN_KV_CHUNK), ATTN_KV_CHUNK), :]

    def v_chunk(c):
        return v_ref[pl.ds(pl.multiple_of(c * ATTN_KV_CHUNK, ATTN_KV_CHUNK), ATTN_KV_CHUNK), :]

    o = _attn_steps(qs, k_chunk, v_chunk, lo, n_chunks, m_ref, l_ref, acc_ref)
    for g in range(GQA_GROUP):
        o_ref[:, g * d:(g + 1) * d] = o[g * tq:(g + 1) * tq].astype(o_ref.dtype)


def _gqa_attention(qk, v, n_lat):
    b, t, _ = qk.shape
    d = GQA_HEAD_DIM
    hkv = v.shape[2] // d
    hq = hkv * GQA_GROUP
    gw = GQA_GROUP * d
    kern = functools.partial(_gqa_attn_kernel, n_lat_tiles=n_lat // ROW_TILE,
                             n_chunks=t // ATTN_KV_CHUNK, n_lat_chunks=n_lat // ATTN_KV_CHUNK)
    rows = GQA_GROUP * ROW_TILE
    return pl.pallas_call(
        kern,
        grid=(b, hkv, t // ROW_TILE),
        in_specs=[pl.BlockSpec((None, ROW_TILE, gw), lambda i, h, j: (i, j, h)),
                  pl.BlockSpec((None, t, d), lambda i, h, j: (i, 0, hq + h)),
                  pl.BlockSpec((None, t, d), lambda i, h, j: (i, 0, h))],
        out_specs=pl.BlockSpec((None, ROW_TILE, gw), lambda i, h, j: (i, j, h)),
        out_shape=jax.ShapeDtypeStruct((b, t, hq * d), BF16),
        scratch_shapes=[pltpu.VMEM((rows, 1), F32), pltpu.VMEM((rows, 1), F32), pltpu.VMEM((rows, d), F32)],
        compiler_params=_params("arbitrary", "arbitrary", "arbitrary"),
        name="gqa_attention",
    )(qk, qk, v)


def _mla_attn_kernel(qn_ref, qr_ref, kn_ref, kr_ref, v_ref, o_ref, m_ref, l_ref, acc_ref, *,
                     n_lat_tiles, n_chunks, n_lat_chunks):
    is_ctx = pl.program_id(2) >= n_lat_tiles
    lo = jnp.where(is_ctx, n_lat_chunks, 0)
    odd = pl.program_id(1) % 2
    qr = qr_ref[...]
    lane = lax.broadcasted_iota(jnp.int32, qr.shape, 1)
    qr = jnp.where((lane // MLA_ROPE) == odd, qr, jnp.zeros_like(qr))
    qs = jnp.concatenate([qn_ref[...], qr], axis=1)

    def k_chunk(c):
        sl = pl.ds(pl.multiple_of(c * ATTN_KV_CHUNK, ATTN_KV_CHUNK), ATTN_KV_CHUNK)
        return jnp.concatenate([kn_ref[sl, :], kr_ref[sl, :]], axis=1)

    def v_chunk(c):
        return v_ref[pl.ds(pl.multiple_of(c * ATTN_KV_CHUNK, ATTN_KV_CHUNK), ATTN_KV_CHUNK), :]

    o = _attn_steps(qs, k_chunk, v_chunk, lo, n_chunks, m_ref, l_ref, acc_ref)
    o_ref[...] = o.astype(o_ref.dtype)


def _mla_attention(q, kv, kr, n_lat, heads):
    b, t, _ = q.shape
    kern = functools.partial(_mla_attn_kernel, n_lat_tiles=n_lat // ROW_TILE,
                             n_chunks=t // ATTN_KV_CHUNK, n_lat_chunks=n_lat // ATTN_KV_CHUNK)
    return pl.pallas_call(
        kern,
        grid=(b, heads, t // ROW_TILE),
        in_specs=[pl.BlockSpec((None, ROW_TILE, MLA_NOPE), lambda i, h, j: (i, j, h)),
                  pl.BlockSpec((None, ROW_TILE, LANES), lambda i, h, j: (i, j, heads + h // 2)),
                  pl.BlockSpec((None, t, MLA_NOPE), lambda i, h, j: (i, 0, h)),
                  pl.BlockSpec((None, t, LANES), lambda i, h, j: (i, 0, 0)),
                  pl.BlockSpec((None, t, MLA_V), lambda i, h, j: (i, 0, heads + h))],
        out_specs=pl.BlockSpec((None, ROW_TILE, MLA_V), lambda i, h, j: (i, j, h)),
        out_shape=jax.ShapeDtypeStruct((b, t, heads * MLA_V), BF16),
        scratch_shapes=[pltpu.VMEM((ROW_TILE, 1), F32), pltpu.VMEM((ROW_TILE, 1), F32),
                        pltpu.VMEM((ROW_TILE, MLA_V), F32)],
        compiler_params=_params("arbitrary", "arbitrary", "arbitrary"),
        name="mla_attention",
    )(q, q, kv, kr, kv)


def _chunk_order(step, n_lat_chunks, n_chunks, reverse):
    if reverse:
        return n_chunks - 1 - step
    n_ctx = n_chunks - n_lat_chunks
    return jnp.where(step < n_ctx, n_lat_chunks + step, step - n_ctx)


def _ret_kernel(lg_ref, q_ref, k_ref, v_ref, *rest, reverse):
    if reverse:
        of_ref, g_ref, o_ref, s_ref, mask_ref, qd_ref, kd_ref = rest
    else:
        o_ref, s_ref, mask_ref, qd_ref, kd_ref = rest
    c = RET_CHUNK
    lg = lg_ref[pl.program_id(1)]

    @pl.when(pl.program_id(2) == 0)
    def _():
        s_ref[...] = jnp.zeros(s_ref.shape, F32)
        i = lax.broadcasted_iota(jnp.int32, (c, c), 0)
        j = lax.broadcasted_iota(jnp.int32, (c, c), 1)
        diff = (j - i) if reverse else (i - j)
        mask_ref[...] = jnp.where(diff >= 0, jnp.exp(jnp.maximum(diff, 0).astype(F32) * lg), 0.0)
        pos = lax.broadcasted_iota(jnp.int32, qd_ref.shape, 0)
        pos = (c - 1 - pos) if reverse else pos
        qd_ref[...] = jnp.exp((pos + 1).astype(F32) * lg)
        kd_ref[...] = jnp.exp((c - 1 - pos).astype(F32) * lg)

    q = q_ref[...]
    k = k_ref[...]
    v = v_ref[...]
    s_old = s_ref[...]
    scores = _dot_nt(q, k) * mask_ref[...]
    o = _dot(scores.astype(BF16), v) + _dot((q.astype(F32) * qd_ref[...]).astype(BF16), s_old.astype(BF16))
    chunk_decay = jnp.exp(jnp.zeros((1, s_old.shape[1]), F32) + float(c) * lg)
    s_ref[...] = s_old * chunk_decay + _dot_tn((k.astype(F32) * kd_ref[...]).astype(BF16), v)
    if reverse:
        o = o + of_ref[...]
        o = o * lax.rsqrt(jnp.mean(o * o, axis=-1, keepdims=True) + EPS)
        o_ref[...] = (o * _silu(g_ref[...].astype(F32))).astype(o_ref.dtype)
    else:
        o_ref[...] = o


def _retention(qk, vg, lg_fwd, lg_bwd, n_lat):
    b, t, _ = qk.shape
    heads = qk.shape[2] // (2 * RET_DK)
    c = RET_CHUNK
    n_chunks, n_lat_chunks = t // c, n_lat // c

    def call(reverse, lg, extra_args, extra_specs, out_dtype):
        order = functools.partial(_chunk_order, n_lat_chunks=n_lat_chunks, n_chunks=n_chunks, reverse=reverse)
        in_specs = [pl.BlockSpec(memory_space=pltpu.SMEM),
                    pl.BlockSpec((None, c, RET_DK), lambda i, h, s: (i, order(s), h)),
                    pl.BlockSpec((None, c, RET_DK), lambda i, h, s: (i, order(s), heads + h)),
                    pl.BlockSpec((None, c, RET_DV), lambda i, h, s: (i, order(s), h))]
        in_specs += [spec(order) for spec in extra_specs]
        return pl.pallas_call(
            functools.partial(_ret_kernel, reverse=reverse),
            grid=(b, heads, n_chunks),
            in_specs=in_specs,
            out_specs=pl.BlockSpec((None, c, RET_DV), lambda i, h, s: (i, order(s), h)),
            out_shape=jax.ShapeDtypeStruct((b, t, heads * RET_DV), out_dtype),
            scratch_shapes=[pltpu.VMEM((RET_DK, RET_DV), F32), pltpu.VMEM((c, c), F32),
                            pltpu.VMEM((c, RET_DK), F32), pltpu.VMEM((c, RET_DK), F32)],
            compiler_params=_params("arbitrary", "arbitrary", "arbitrary"),
            name="retention_bwd" if reverse else "retention_fwd",
        )(lg, qk, qk, vg, *extra_args)

    o_f = call(False, lg_fwd, (), (), F32)
    of_spec = lambda order: pl.BlockSpec((None, c, RET_DV), lambda i, h, s: (i, order(s), h))
    g_spec = lambda order: pl.BlockSpec((None, c, RET_DV), lambda i, h, s: (i, order(s), heads + h))
    return call(True, lg_bwd, (o_f, vg), (of_spec, g_spec), BF16)


def _gla_kernel(q_ref, k_ref, lf_ref, v_ref, *rest, reverse, heads):
    if reverse:
        of_ref, g_ref, gain_ref, o_ref, st_ref = rest
    else:
        o_ref, st_ref = rest
    c = HGRN_CHUNK
    d = HGRN_DIM

    @pl.when(pl.program_id(1) == 0)
    def _():
        st_ref[...] = jnp.zeros(st_ref.shape, F32)

    i = lax.broadcasted_iota(jnp.int32, (c, c), 0)
    j = lax.broadcasted_iota(jnp.int32, (c, c), 1)
    tri = (j >= i) if reverse else (i >= j)
    tri_bf = jnp.where(tri, 1.0, 0.0).astype(BF16)
    lf = lf_ref[...]
    lf_hi = lf.astype(BF16)
    lf_lo = (lf - lf_hi.astype(F32)).astype(BF16)
    bcum = _dot(tri_bf, lf_hi) + _dot(tri_bf, lf_lo)
    b_last = bcum[0:1, :] if reverse else bcum[c - 1:c, :]
    q_in = q_ref[...].astype(F32) * jnp.exp(bcum)
    kf = k_ref[...].astype(F32)
    k_in = (kf * jnp.exp(-bcum)).astype(BF16)
    k_out = (kf * jnp.exp(b_last - bcum)).astype(BF16)
    decay = jnp.exp(b_last)
    q_in = q_in.astype(BF16)
    v = v_ref[...]
    for h in range(heads):
        sl = slice(h * d, (h + 1) * d)
        st = st_ref[h]
        scores = jnp.where(tri, _dot_nt(q_in[:, sl], k_in[:, sl]), 0.0)
        o = _dot_nt(q_in[:, sl], st.astype(BF16)) + _dot(scores.astype(BF16), v[:, sl])
        st_ref[h] = st * decay[:, sl] + _dot_tn(v[:, sl], k_out[:, sl])
        if reverse:
            o = o + of_ref[:, sl]
            o = o * lax.rsqrt(jnp.mean(o * o, axis=-1, keepdims=True) + EPS) * gain_ref[...]
            o_ref[:, sl] = (o * _silu(g_ref[:, sl].astype(F32))).astype(o_ref.dtype)
        else:
            o_ref[:, sl] = o


def _gla(qs, kf, lff, kb, lfb, ig, gain, n_lat, n_out_tiles=None):
    b, t, w = qs.shape
    heads = w // HGRN_DIM
    c = HGRN_CHUNK
    n_chunks, n_lat_chunks = t // c, n_lat // c

    def call(reverse, k, lf, extra_args, extra_specs, out_dtype):
        order = functools.partial(_chunk_order, n_lat_chunks=n_lat_chunks, n_chunks=n_chunks, reverse=reverse)
        blk = lambda col: pl.BlockSpec((None, c, w), lambda i, s: (i, order(s), col))
        return pl.pallas_call(
            functools.partial(_gla_kernel, reverse=reverse, heads=heads),
            grid=(b, n_chunks),
            in_specs=[blk(0), blk(0), blk(0), blk(0)] + [spec(blk) for spec in extra_specs],
            out_specs=blk(0),
            out_shape=jax.ShapeDtypeStruct((b, t, w), out_dtype),
            scratch_shapes=[pltpu.VMEM((heads, HGRN_DIM, HGRN_DIM), F32)],
            compiler_params=_params("arbitrary", "arbitrary"),
            name="gla_bwd" if reverse else "gla_fwd",
        )(qs, k, lf, ig, *extra_args)

    o_f = call(False, kf, lff, (), (), F32)
    gain_spec = lambda blk: pl.BlockSpec((1, HGRN_DIM), lambda i, s: (0, 0))
    return call(True, kb, lfb, (o_f, ig, gain.reshape(1, HGRN_DIM)),
                (lambda blk: blk(0), lambda blk: blk(1), gain_spec), BF16)


def _rope_tables(rot_dim, n_lat, n_ctx, batch, width):
    half = rot_dim // 2
    freqs = ROPE_THETA ** (-jnp.arange(0, half, 2, dtype=F32) / half)
    tok = jnp.arange(n_lat)
    a_row = (tok // GRID_W).astype(F32)[:, None] * freqs
    a_col = (tok % GRID_W).astype(F32)[:, None] * freqs
    cos = jnp.concatenate([jnp.cos(a_row), jnp.cos(a_row), jnp.cos(a_col), jnp.cos(a_col)], axis=-1)
    sin = jnp.concatenate([-jnp.sin(a_row), jnp.sin(a_row), -jnp.sin(a_col), jnp.sin(a_col)], axis=-1)
    cos = jnp.concatenate([cos, jnp.ones((n_ctx, rot_dim), F32)], axis=0)
    sin = jnp.concatenate([sin, jnp.zeros((n_ctx, rot_dim), F32)], axis=0)
    reps = (batch, width // rot_dim)
    return jnp.tile(cos, reps), jnp.tile(sin, reps)


MM_ROW_TILE = 1024
MM_COL_TILES = (1024, 512, 256, 128)


def _col_tile(*widths):
    return next(tn for tn in MM_COL_TILES if all(w % tn == 0 for w in widths))


def _retention_mixer(h2, w_in, lg_fwd, lg_bwd, b, t, n_lat):
    heads = lg_fwd.shape[0]
    hk = heads * RET_DK
    w = w_in.astype(BF16)
    cos, sin = _rope_tables(RET_DK, n_lat, t - n_lat, b, RET_DK)
    gain = jnp.concatenate([jnp.full((1, hk), RET_DK ** -0.5, F32), jnp.ones((1, hk), F32)], axis=1)
    tn = _col_tile(2 * hk, w.shape[1] - 2 * hk)
    qk, = _mm_call(functools.partial(_rope_body, swap=RET_DK // 4, norm=False), h2, w,
                   col_blk0=0, n_blk=2 * hk // tn, tn=tn, tm=MM_ROW_TILE, out_dtypes=[BF16],
                   row_ins=(cos, sin), col_ins=(gain,), name="ret_in_qk")
    vg, = _mm_call(functools.partial(_plain_body, scale=1.0), h2, w,
                   col_blk0=2 * hk // tn, n_blk=(w.shape[1] - 2 * hk) // tn, tn=tn, tm=MM_ROW_TILE,
                   out_dtypes=[BF16], name="ret_in_vg")
    return _retention(qk.reshape(b, t, -1), vg.reshape(b, t, -1), lg_fwd, lg_bwd, n_lat)


def _gqa_mixer(h2, w_in, q_gain, k_gain, b, t, n_lat):
    d = GQA_HEAD_DIM
    w = w_in.astype(BF16)
    n_qk = w.shape[1] * (GQA_GROUP + 1) // (GQA_GROUP + 2)
    hq = n_qk // d * GQA_GROUP // (GQA_GROUP + 1)
    cos, sin = _rope_tables(d, n_lat, t - n_lat, b, d)
    gain = jnp.concatenate([jnp.tile(q_gain * d ** -0.5, hq), jnp.tile(k_gain, n_qk // d - hq)])[None, :]
    tn = _col_tile(n_qk, w.shape[1] - n_qk)
    qk, = _mm_call(functools.partial(_rope_body, swap=d // 4, norm=True), h2, w,
                   col_blk0=0, n_blk=n_qk // tn, tn=tn, tm=MM_ROW_TILE, out_dtypes=[BF16],
                   row_ins=(cos, sin), col_ins=(gain.astype(F32),), name="gqa_in_qk")
    v, = _mm_call(functools.partial(_plain_body, scale=1.0), h2, w,
                  col_blk0=n_qk // tn, n_blk=(w.shape[1] - n_qk) // tn, tn=tn, tm=MM_ROW_TILE,
                  out_dtypes=[BF16], name="gqa_in_v")
    return _gqa_attention(qk.reshape(b, t, -1), v.reshape(b, t, -1), n_lat)


def _mla_mixer(h2, w_in, q_norm, w_q_up, kv_norm, w_kv_up, b, t, n_lat):
    qr, kvr = q_norm.shape[0], kv_norm.shape[0]
    heads = w_q_up.shape[1] // (MLA_NOPE + MLA_ROPE)
    rows = h2.shape[0]
    w = jnp.concatenate([w_in, w_in[:, qr + kvr:]], axis=1).astype(BF16)
    cos, sin = _rope_tables(MLA_ROPE, n_lat, t - n_lat, b, LANES)
    tm = MM_ROW_TILE
    row = lambda width: pl.BlockSpec((tm, width), lambda i: (i, 0))
    full = lambda r, c: pl.BlockSpec((r, c), lambda i: (0, 0))
    cq, ckv, kr = pl.pallas_call(
        functools.partial(_mla_in_body, qr=qr, kvr=kvr),
        grid=(rows // tm,),
        in_specs=[row(h2.shape[1]), full(*w.shape), row(LANES), row(LANES), full(1, qr), full(1, kvr)],
        out_specs=[row(qr), row(kvr), row(LANES)],
        out_shape=[jax.ShapeDtypeStruct((rows, qr), BF16), jax.ShapeDtypeStruct((rows, kvr), BF16),
                   jax.ShapeDtypeStruct((rows, LANES), BF16)],
        compiler_params=_params("arbitrary"),
        name="mla_in",
    )(h2, w, cos, sin, q_norm[None, :], kv_norm[None, :])
    wq = w_q_up.reshape(qr, heads, MLA_NOPE + MLA_ROPE)
    wq = jnp.concatenate([wq[:, :, :MLA_NOPE].reshape(qr, -1), wq[:, :, MLA_NOPE:].reshape(qr, -1)], axis=1).astype(BF16)
    wkv = w_kv_up.reshape(kvr, heads, MLA_NOPE + MLA_V)
    wkv = jnp.concatenate([wkv[:, :, :MLA_NOPE].reshape(kvr, -1), wkv[:, :, MLA_NOPE:].reshape(kvr, -1)], axis=1).astype(BF16)
    scale = (MLA_NOPE + MLA_ROPE) ** -0.5
    n_nope = heads * MLA_NOPE
    n_rope = heads * MLA_ROPE
    tn = _col_tile(n_nope, n_rope)
    q_nope, = _mm_call(functools.partial(_plain_body, scale=scale), cq, wq, col_blk0=0, n_blk=n_nope // tn,
                       tn=tn, tm=tm, out_dtypes=[BF16], name="mla_q_nope")
    gain = jnp.full((1, n_rope), scale, F32)
    q_rope, = _mm_call(functools.partial(_rope_body, swap=MLA_ROPE // 4, norm=False), cq, wq,
                       col_blk0=n_nope // tn, n_blk=n_rope // tn, tn=tn, tm=tm, out_dtypes=[BF16],
                       row_ins=(cos, sin), col_ins=(gain,), name="mla_q_rope")
    q = jnp.concatenate([q_nope, q_rope], axis=1)
    tn = _col_tile(wkv.shape[1])
    kv, = _mm_call(functools.partial(_plain_body, scale=1.0), ckv, wkv, col_blk0=0, n_blk=wkv.shape[1] // tn,
                   tn=tn, tm=tm, out_dtypes=[BF16], name="mla_kv_up")
    return _mla_attention(q.reshape(b, t, -1), kv.reshape(b, t, -1), kr.reshape(b, t, -1), n_lat, heads)


def _hgrn_mixer(h2, w_in, lb_raw, out_gain, layer, b, t, n_lat):
    w = w_in.astype(BF16)
    width = w.shape[1] // 5
    tn, tm = _col_tile(width), MM_ROW_TILE
    nb = width // tn
    qs, = _mm_call(_silu_body, h2, w, col_blk0=0, n_blk=nb, tn=tn, tm=tm, out_dtypes=[BF16], name="hgrn_in_q")
    forget = functools.partial(_forget_body, layer=layer)
    kf, lff = _mm_call(forget, h2, w, col_blk0=nb, n_blk=nb, tn=tn, tm=tm, out_dtypes=[BF16, F32],
                       col_ins=(lb_raw,), name="hgrn_in_ff")
    kb, lfb = _mm_call(forget, h2, w, col_blk0=2 * nb, n_blk=nb, tn=tn, tm=tm, out_dtypes=[BF16, F32],
                       col_ins=(lb_raw,), name="hgrn_in_fb")
    ig, = _mm_call(functools.partial(_plain_body, scale=1.0), h2, w, col_blk0=3 * nb, n_blk=2 * nb, tn=tn, tm=tm,
                   out_dtypes=[BF16], name="hgrn_in_ig")
    r3 = lambda a: a.reshape(b, t, -1)
    return _gla(r3(qs), r3(kf), r3(lff), r3(kb), r3(lfb), r3(ig), out_gain, n_lat)


def kernel(x, c, ctx, c_ctx, ada_w, ada_b, ln_g, ln_b, ffn_w_in, ffn_w_out, ret_w_in, ret_decay_fwd, ret_decay_bwd, ret_w_out, gqa_w_in, gqa_q_norm, gqa_k_norm, gqa_w_out, mla_w_in, mla_q_norm, mla_w_q_up, mla_kv_norm, mla_w_kv_up, mla_w_out, hgrn_w_in, hgrn_lb_raw, hgrn_out_norm, hgrn_w_out):
    b, n_lat, d = x.shape
    n_ctx = ctx.shape[1]
    t = n_lat + n_ctx
    depth = ada_w.shape[0]
    alpha = (2 * depth) ** 0.25
    n_lat_tiles = n_lat // ROW_TILE
    n_tiles = t // ROW_TILE

    pad = (-(b + 1)) % 8
    cc = jnp.concatenate([c, c_ctx[None, :], jnp.zeros((pad, d), F32)], axis=0)
    mod = _ada_all(cc, ada_w, ada_b)

    def vec(layer, idx):
        m = mod[layer, :, idx * d:(idx + 1) * d]
        return jnp.stack([m[:b], jnp.broadcast_to(m[b], (b, d))], axis=1)[:, :, None, :]

    xs = jnp.concatenate([x, ctx], axis=1)
    h = _modulate0(xs, vec(0, 0), vec(0, 1), n_lat_tiles)

    for i in range(depth):
        m, j = i % N_MIXERS, i // N_MIXERS
        last = i == depth - 1
        h2 = h.reshape(b * t, d)
        if m == 0:
            a = _retention_mixer(h2, ret_w_in[j], ret_decay_fwd[j], ret_decay_bwd[j], b, t, n_lat)
            w_out = ret_w_out[j]
        elif m == 1:
            a = _gqa_mixer(h2, gqa_w_in[j], gqa_q_norm[j], gqa_k_norm[j], b, t, n_lat)
            w_out = gqa_w_out[j]
        elif m == 2:
            a = _mla_mixer(h2, mla_w_in[j], mla_q_norm[j], mla_w_q_up[j], mla_kv_norm[j], mla_w_kv_up[j], b, t, n_lat)
            w_out = mla_w_out[j]
        else:
            a = _hgrn_mixer(h2, hgrn_w_in[j], hgrn_lb_raw, hgrn_out_norm[j], i, b, t, n_lat)
            w_out = hgrn_w_out[j]
        xs, h = _outproj_ln(a, w_out.astype(BF16), xs, vec(i, 2), ln_g[i, 0][None, :], ln_b[i, 0][None, :],
                            vec(i, 3), vec(i, 4), alpha=alpha, n_lat_tiles=n_lat_tiles, n_tiles=n_tiles,
                            emit_h=True, name="mixer_out_ln")
        act = _ffn_in(h.reshape(b * t, d), ffn_w_in[i].astype(BF16), MM_ROW_TILE, 512).reshape(b, t, -1)
        if last:
            xs, = _outproj_ln(act, ffn_w_out[i].astype(BF16), xs, vec(i, 5), ln_g[i, 1][None, :], ln_b[i, 1][None, :],
                              vec(i, 5), vec(i, 5), alpha=alpha, n_lat_tiles=n_lat_tiles, n_tiles=n_lat_tiles,
                              emit_h=False, name="ffn_out_ln_final")
        else:
            xs, h = _outproj_ln(act, ffn_w_out[i].astype(BF16), xs, vec(i, 5), ln_g[i, 1][None, :], ln_b[i, 1][None, :],
                                vec(i + 1, 0), vec(i + 1, 1), alpha=alpha, n_lat_tiles=n_lat_tiles, n_tiles=n_tiles,
                                emit_h=True, name="ffn_out_ln")
    return xs
```

```python
import functools

import jax
import jax.numpy as jnp
from jax import lax
from jax.experimental import pallas as pl
from jax.experimental.pallas import tpu as pltpu

F32 = jnp.float32
BF16 = jnp.bfloat16

GRID_W = 64
N_MIXERS = 4
RET_DK = 256
RET_DV = 512
RET_CHUNK = 128
GQA_HEAD_DIM = 128
GQA_GROUP = 4
MLA_NOPE = 128
MLA_ROPE = 64
MLA_V = 128
HGRN_DIM = 128
HGRN_CHUNK = 64
ROPE_THETA = 10000.0
EPS = 1e-6

LANES = 128
ROW_TILE = 256
ATTN_KV_CHUNKS = (512, 256)
LOG2E = 1.4426950408889634
V7X_VMEM_LIMIT = 56 * 1024 * 1024


def _params(*sem):
    return pltpu.CompilerParams(dimension_semantics=sem, vmem_limit_bytes=V7X_VMEM_LIMIT)


def _dot(a, b):
    return jnp.dot(a, b, preferred_element_type=F32)


def _dot_nt(a, b):
    return lax.dot_general(a, b, (((1,), (1,)), ((), ())), preferred_element_type=F32)


def _dot_tn(a, b):
    return lax.dot_general(a, b, (((0,), (0,)), ((), ())), preferred_element_type=F32)


def _silu(x):
    return x * jax.nn.sigmoid(x)


def _swap_halves(x, w):
    if 2 * w == LANES:
        return pltpu.roll(x, w, axis=1)
    up = pltpu.roll(x, LANES - w, axis=1)
    down = pltpu.roll(x, w, axis=1)
    lane = lax.broadcasted_iota(jnp.int32, x.shape, 1)
    return jnp.where((lane % (2 * w)) < w, up, down)


def _ada_kernel(c_ref, w_ref, b_ref, o_ref):
    s = _silu(c_ref[...]).astype(BF16)
    o_ref[...] = _dot(s, w_ref[...].astype(BF16)) + b_ref[...]


def _ada_all(cc, ada_w, ada_b, tn=1024):
    depth, d, n = ada_w.shape
    rows = cc.shape[0]
    return pl.pallas_call(
        _ada_kernel,
        grid=(depth, n // tn),
        in_specs=[pl.BlockSpec((rows, d), lambda l, j: (0, 0)),
                  pl.BlockSpec((None, d, tn), lambda l, j: (l, 0, j)),
                  pl.BlockSpec((None, 1, tn), lambda l, j: (l, 0, j))],
        out_specs=pl.BlockSpec((None, rows, tn), lambda l, j: (l, 0, j)),
        out_shape=jax.ShapeDtypeStruct((depth, rows, n), F32),
        compiler_params=_params("arbitrary", "arbitrary"),
        name="ada_mod",
    )(cc, ada_w, ada_b.reshape(depth, 1, n))


def _mod_kernel(x_ref, sh_ref, sc_ref, h_ref):
    h_ref[...] = (x_ref[...] * (1.0 + sc_ref[...]) + sh_ref[...]).astype(BF16)


def _modulate0(x, sh, sc, n_lat_tiles):
    b, t, d = x.shape
    vec = pl.BlockSpec((None, None, 1, d), lambda i, j: (i, j // n_lat_tiles, 0, 0))
    return pl.pallas_call(
        _mod_kernel,
        grid=(b, t // ROW_TILE),
        in_specs=[pl.BlockSpec((None, ROW_TILE, d), lambda i, j: (i, j, 0)), vec, vec],
        out_specs=pl.BlockSpec((None, ROW_TILE, d), lambda i, j: (i, j, 0)),
        out_shape=jax.ShapeDtypeStruct((b, t, d), BF16),
        compiler_params=_params("arbitrary", "arbitrary"),
        name="modulate0",
    )(x, sh, sc)


def _mm_call(body, h, w, *, col_blk0, n_blk, tn, tm, out_dtypes, name, row_ins=(), col_ins=()):
    rows, k = h.shape
    assert rows % tm == 0 and w.shape[0] == k
    in_specs = [pl.BlockSpec((tm, k), lambda j, i: (i, 0)),
                pl.BlockSpec((k, tn), lambda j, i: (0, j + col_blk0))]
    args = [h, w]
    for a in row_ins:
        in_specs.append(pl.BlockSpec((tm, a.shape[1]), lambda j, i: (i, 0)))
        args.append(a)
    for a in col_ins:
        assert a.shape[1] == n_blk * tn
        in_specs.append(pl.BlockSpec((a.shape[0], tn), lambda j, i: (0, j)))
        args.append(a)
    out = pl.pallas_call(
        body,
        grid=(n_blk, rows // tm),
        in_specs=in_specs,
        out_specs=[pl.BlockSpec((tm, tn), lambda j, i: (i, j)) for _ in out_dtypes],
        out_shape=[jax.ShapeDtypeStruct((rows, n_blk * tn), dt) for dt in out_dtypes],
        compiler_params=_params("arbitrary", "arbitrary"),
        name=name,
    )(*args)
    return out


def _plain_body(h_ref, w_ref, o_ref, *, scale):
    acc = _dot(h_ref[...], w_ref[...])
    if scale != 1.0:
        acc = acc * scale
    o_ref[...] = acc.astype(o_ref.dtype)


def _silu_body(h_ref, w_ref, o_ref):
    o_ref[...] = _silu(_dot(h_ref[...], w_ref[...])).astype(o_ref.dtype)


def _rope_body(h_ref, w_ref, cos_ref, sin_ref, gain_ref, o_ref, *, swap, norm):
    acc = _dot(h_ref[...], w_ref[...])
    tn = acc.shape[1]
    tw = cos_ref.shape[1]
    for c in range(tn // LANES):
        sl = slice(c * LANES, (c + 1) * LANES)
        x = acc[:, sl]
        if norm:
            x = x * lax.rsqrt(jnp.mean(x * x, axis=-1, keepdims=True) + EPS)
        x = x * gain_ref[:, sl]
        t = c % (tw // LANES)
        tsl = slice(t * LANES, (t + 1) * LANES)
        y = x * cos_ref[:, tsl] + _swap_halves(x, swap) * sin_ref[:, tsl]
        o_ref[:, sl] = y.astype(o_ref.dtype)


def _forget_body(h_ref, w_ref, raw_ref, k_ref, lf_ref, *, layer):
    raw = raw_ref[...]
    e = jnp.exp(raw - jnp.max(raw, axis=0, keepdims=True))
    p = e / jnp.sum(e, axis=0, keepdims=True)
    lb = jnp.zeros_like(p[0:1])
    for r in range(1, layer + 1):
        lb = lb + p[r:r + 1]
    f = lb + (1.0 - lb) * jax.nn.sigmoid(_dot(h_ref[...], w_ref[...]))
    k_ref[...] = (1.0 - f).astype(k_ref.dtype)
    lf_ref[...] = jnp.log(f)


def _mla_in_body(h_ref, w_ref, cos_ref, sin_ref, qn_ref, kvn_ref, cq_ref, ckv_ref, kr_ref, *, qr, kvr):
    acc = _dot(h_ref[...], w_ref[...])
    cq = acc[:, :qr]
    cq_ref[...] = (cq * lax.rsqrt(jnp.mean(cq * cq, axis=-1, keepdims=True) + EPS) * qn_ref[...]).astype(BF16)
    ckv = acc[:, qr:qr + kvr]
    ckv_ref[...] = (ckv * lax.rsqrt(jnp.mean(ckv * ckv, axis=-1, keepdims=True) + EPS) * kvn_ref[...]).astype(BF16)
    kr = acc[:, qr + kvr:]
    kr_ref[...] = (kr * cos_ref[...] + _swap_halves(kr, MLA_ROPE // 4) * sin_ref[...]).astype(BF16)


def _swiglu_body(h_ref, wa_ref, wb_ref, o_ref):
    h = h_ref[...]
    a = _dot(h, wa_ref[...])
    b = _dot(h, wb_ref[...])
    o_ref[...] = (_silu(a) * b).astype(o_ref.dtype)


def _ffn_in(h, w, tm, tn):
    rows, k = h.shape
    hidden = w.shape[1] // 2
    nb = hidden // tn
    return pl.pallas_call(
        _swiglu_body,
        grid=(nb, rows // tm),
        in_specs=[pl.BlockSpec((tm, k), lambda j, i: (i, 0)),
                  pl.BlockSpec((k, tn), lambda j, i: (0, j)),
                  pl.BlockSpec((k, tn), lambda j, i: (0, j + nb))],
        out_specs=pl.BlockSpec((tm, tn), lambda j, i: (i, j)),
        out_shape=jax.ShapeDtypeStruct((rows, hidden), BF16),
        compiler_params=_params("arbitrary", "arbitrary"),
        name="ffn_in_swiglu",
    )(h, w, w)


def _outproj_ln_kernel(a_ref, w_ref, x_ref, gate_ref, lng_ref, lnb_ref, sh_ref, sc_ref, *out_refs,
                       alpha, emit_h):
    y = _dot(a_ref[...], w_ref[...])
    z = alpha * x_ref[...] + gate_ref[...] * y
    zc = z - jnp.mean(z, axis=-1, keepdims=True)
    var = jnp.mean(zc * zc, axis=-1, keepdims=True)
    xn = zc * lax.rsqrt(var + EPS) * lng_ref[...] + lnb_ref[...]
    out_refs[0][...] = xn
    if emit_h:
        out_refs[1][...] = (xn * (1.0 + sc_ref[...]) + sh_ref[...]).astype(BF16)


def _outproj_ln(a, w, x, gate, lng, lnb, sh, sc, *, alpha, n_lat_tiles, n_tiles, emit_h, name):
    b, _, k = a.shape
    d = w.shape[1]
    tile = lambda width: pl.BlockSpec((None, ROW_TILE, width), lambda i, j: (i, j, 0))
    vec = pl.BlockSpec((None, None, 1, d), lambda i, j: (i, j // n_lat_tiles, 0, 0))
    full = pl.BlockSpec((1, d), lambda i, j: (0, 0))
    out_shape = [jax.ShapeDtypeStruct((b, n_tiles * ROW_TILE, d), F32)]
    out_specs = [tile(d)]
    if emit_h:
        out_shape.append(jax.ShapeDtypeStruct((b, n_tiles * ROW_TILE, d), BF16))
        out_specs.append(tile(d))
    return pl.pallas_call(
        functools.partial(_outproj_ln_kernel, alpha=alpha, emit_h=emit_h),
        grid=(b, n_tiles),
        in_specs=[tile(k),
                  pl.BlockSpec((k, d), lambda i, j: (0, 0), pipeline_mode=pl.Buffered(1)),
                  tile(d), vec, full, full, vec, vec],
        out_specs=out_specs,
        out_shape=out_shape,
        compiler_params=_params("arbitrary", "arbitrary"),
        name=name,
    )(a, w, x, gate, lng, lnb, sh, sc)


def _kv_chunk(n_lat):
    return next(ck for ck in ATTN_KV_CHUNKS if n_lat % ck == 0)


def _fill_vt(v, vt_ref, vtc_ref, n_lat, ck):
    dv = v.shape[1]
    vt = v.astype(F32).T.astype(BF16)
    for c in range(n_lat // ck):
        vt_ref[c, 0:dv, :] = vt[:, c * ck:(c + 1) * ck]
        vt_ref[c, dv:, :] = jnp.ones((vt_ref.shape[1] - dv, ck), BF16)
    vtc_ref[0:dv, :] = vt[:, n_lat:]
    vtc_ref[dv:, :] = jnp.ones((vtc_ref.shape[0] - dv, vtc_ref.shape[1]), BF16)


def _attend(streams, is_ctx, n_chunks, m_ref, acc_ref, sa_ref, sb_ref):
    assert n_chunks % 2 == 0
    n = len(streams)
    dv = acc_ref.shape[1] // 2
    m_ref[...] = jnp.full(m_ref.shape, -jnp.inf, F32)
    acc_ref[...] = jnp.zeros(acc_ref.shape, F32)

    def accumulate(h, s, vtc):
        m_old = m_ref[h]
        m_new = jnp.maximum(m_old, jnp.max(s, axis=0, keepdims=True))
        p = jnp.exp2(s - m_new).astype(BF16)
        acc_ref[h] = jnp.exp2(m_old - m_new) * acc_ref[h] + _dot(vtc, p)
        m_ref[h] = m_new

    def stage(c_next, buf_next, c_cur, buf_cur):
        for h, (qs, k_lat, _, vt_ref, _) in enumerate(streams):
            if c_next is not None:
                buf_next[h] = _dot_nt(k_lat(c_next), qs)
            if c_cur is not None:
                accumulate(h, buf_cur[h], vt_ref[c_cur])

    @pl.when(jnp.logical_not(is_ctx))
    def _():
        stage(0, sa_ref, None, None)

        def pair(i, carry):
            stage(2 * i + 1, sb_ref, 2 * i, sa_ref)
            stage(2 * i + 2, sa_ref, 2 * i + 1, sb_ref)
            return carry

        lax.fori_loop(0, n_chunks // 2 - 1, pair, 0)
        stage(n_chunks - 1, sb_ref, n_chunks - 2, sa_ref)
        stage(None, None, n_chunks - 1, sb_ref)

    outs = []
    for h, (qs, _, k_ctx, _, vtc_ref) in enumerate(streams):
        accumulate(h, _dot_nt(k_ctx, qs), vtc_ref[...])
        acc = acc_ref[h]
        outs.append(acc[:dv] / acc[dv:])
    return outs


def _gqa_attn_kernel(q_ref, k_ref, v_ref, o_ref, vt_ref, vtc_ref, m_ref, acc_ref, sa_ref, sb_ref, *, n_lat, ck):
    d = GQA_HEAD_DIM
    tq = q_ref.shape[0]

    @pl.when(pl.program_id(2) == 0)
    def _():
        _fill_vt(v_ref[...], vt_ref, vtc_ref, n_lat, ck)

    is_ctx = pl.program_id(2) >= n_lat // tq
    q = q_ref[...]
    qs = jnp.concatenate([q[:, g * d:(g + 1) * d] for g in range(GQA_GROUP)], axis=0)
    k_lat = lambda c: k_ref[pl.ds(pl.multiple_of(c * ck, ck), ck), :]
    o_t, = _attend([(qs, k_lat, k_ref[n_lat:, :], vt_ref, vtc_ref)], is_ctx, n_lat // ck,
                   m_ref, acc_ref, sa_ref, sb_ref)
    for g in range(GQA_GROUP):
        o_ref[:, g * d:(g + 1) * d] = o_t[:, g * tq:(g + 1) * tq].T.astype(o_ref.dtype)


def _gqa_attention(qk, v, n_lat):
    b, t, _ = qk.shape
    d = GQA_HEAD_DIM
    hkv = v.shape[2] // d
    hq = hkv * GQA_GROUP
    gw = GQA_GROUP * d
    ck = _kv_chunk(n_lat)
    rows = GQA_GROUP * ROW_TILE
    return pl.pallas_call(
        functools.partial(_gqa_attn_kernel, n_lat=n_lat, ck=ck),
        grid=(b, hkv, t // ROW_TILE),
        in_specs=[pl.BlockSpec((None, ROW_TILE, gw), lambda i, h, j: (i, j, h)),
                  pl.BlockSpec((None, t, d), lambda i, h, j: (i, 0, hq + h)),
                  pl.BlockSpec((None, t, d), lambda i, h, j: (i, 0, h))],
        out_specs=pl.BlockSpec((None, ROW_TILE, gw), lambda i, h, j: (i, j, h)),
        out_shape=jax.ShapeDtypeStruct((b, t, hq * d), BF16),
        scratch_shapes=[pltpu.VMEM((n_lat // ck, 2 * d, ck), BF16), pltpu.VMEM((2 * d, t - n_lat), BF16),
                        pltpu.VMEM((1, 1, rows), F32), pltpu.VMEM((1, 2 * d, rows), F32),
                        pltpu.VMEM((1, ck, rows), F32), pltpu.VMEM((1, ck, rows), F32)],
        compiler_params=_params("arbitrary", "arbitrary", "arbitrary"),
        name="gqa_attention",
    )(qk, qk, v)


MLA_HEAD_GROUP = 4


def _mla_attn_kernel(qn_ref, qr_ref, kn_ref, kr_ref, v_ref, o_ref, vt_ref, vtc_ref, m_ref, acc_ref, sa_ref, sb_ref, *,
                     n_lat, ck):
    tq = qn_ref.shape[0]
    n_chunks = n_lat // ck

    @pl.when(pl.program_id(2) == 0)
    def _():
        for hh in range(MLA_HEAD_GROUP):
            _fill_vt(v_ref[:, hh * MLA_V:(hh + 1) * MLA_V], vt_ref.at[pl.ds(hh * n_chunks, n_chunks)],
                     vtc_ref.at[hh], n_lat, ck)

    is_ctx = pl.program_id(2) >= n_lat // tq
    lane = lax.broadcasted_iota(jnp.int32, (tq, LANES), 1)
    streams = []
    for hh in range(MLA_HEAD_GROUP):
        nope = slice(hh * MLA_NOPE, (hh + 1) * MLA_NOPE)
        pair = qr_ref[:, (hh // 2) * LANES:(hh // 2 + 1) * LANES]
        qr = jnp.where((lane // MLA_ROPE) == (hh % 2), pair, jnp.zeros_like(pair))
        qs = jnp.concatenate([qn_ref[:, nope], qr], axis=1)

        def k_lat(c, nope=nope):
            sl = pl.ds(pl.multiple_of(c * ck, ck), ck)
            return jnp.concatenate([kn_ref[sl, nope], kr_ref[sl, :]], axis=1)

        k_ctx = jnp.concatenate([kn_ref[n_lat:, nope], kr_ref[n_lat:, :]], axis=1)
        streams.append((qs, k_lat, k_ctx, vt_ref.at[pl.ds(hh * n_chunks, n_chunks)], vtc_ref.at[hh]))
    outs = _attend(streams, is_ctx, n_chunks, m_ref, acc_ref, sa_ref, sb_ref)
    for hh, o_t in enumerate(outs):
        o_ref[:, hh * MLA_V:(hh + 1) * MLA_V] = o_t.T.astype(o_ref.dtype)


def _mla_attention(q, kv, kr, n_lat, heads):
    b, t, _ = q.shape
    hg = MLA_HEAD_GROUP
    ck = _kv_chunk(n_lat)
    n_groups = heads // hg
    rope_w = hg * MLA_ROPE
    return pl.pallas_call(
        functools.partial(_mla_attn_kernel, n_lat=n_lat, ck=ck),
        grid=(b, n_groups, t // ROW_TILE),
        in_specs=[pl.BlockSpec((None, ROW_TILE, hg * MLA_NOPE), lambda i, h, j: (i, j, h)),
                  pl.BlockSpec((None, ROW_TILE, rope_w), lambda i, h, j: (i, j, heads * MLA_NOPE // rope_w + h)),
                  pl.BlockSpec((None, t, hg * MLA_NOPE), lambda i, h, j: (i, 0, h)),
                  pl.BlockSpec((None, t, LANES), lambda i, h, j: (i, 0, 0)),
                  pl.BlockSpec((None, t, hg * MLA_V), lambda i, h, j: (i, 0, n_groups + h))],
        out_specs=pl.BlockSpec((None, ROW_TILE, hg * MLA_V), lambda i, h, j: (i, j, h)),
        out_shape=jax.ShapeDtypeStruct((b, t, heads * MLA_V), BF16),
        scratch_shapes=[pltpu.VMEM((hg * (n_lat // ck), 2 * MLA_V, ck), BF16),
                        pltpu.VMEM((hg, 2 * MLA_V, t - n_lat), BF16),
                        pltpu.VMEM((hg, 1, ROW_TILE), F32), pltpu.VMEM((hg, 2 * MLA_V, ROW_TILE), F32),
                        pltpu.VMEM((hg, ck, ROW_TILE), F32), pltpu.VMEM((hg, ck, ROW_TILE), F32)],
        compiler_params=_params("arbitrary", "arbitrary", "arbitrary"),
        name="mla_attention",
    )(q, q, kv, kr, kv)


def _chunk_order(step, n_lat_chunks, n_chunks, reverse):
    if reverse:
        return n_chunks - 1 - step
    n_ctx = n_chunks - n_lat_chunks
    return jnp.where(step < n_ctx, n_lat_chunks + step, step - n_ctx)


def _ret_kernel(lg_ref, q_ref, k_ref, v_ref, *rest, reverse):
    if reverse:
        of_ref, g_ref, o_ref, s_ref, mask_ref, qd_ref, kd_ref = rest
    else:
        o_ref, s_ref, mask_ref, qd_ref, kd_ref = rest
    c = RET_CHUNK
    lg = lg_ref[pl.program_id(1)]

    @pl.when(pl.program_id(2) == 0)
    def _():
        s_ref[...] = jnp.zeros(s_ref.shape, F32)
        i = lax.broadcasted_iota(jnp.int32, (c, c), 0)
        j = lax.broadcasted_iota(jnp.int32, (c, c), 1)
        diff = (j - i) if reverse else (i - j)
        mask_ref[...] = jnp.where(diff >= 0, jnp.exp(jnp.maximum(diff, 0).astype(F32) * lg), 0.0)
        pos = lax.broadcasted_iota(jnp.int32, qd_ref.shape, 0)
        pos = (c - 1 - pos) if reverse else pos
        qd_ref[...] = jnp.exp((pos + 1).astype(F32) * lg)
        kd_ref[...] = jnp.exp((c - 1 - pos).astype(F32) * lg)

    q = q_ref[...]
    k = k_ref[...]
    v = v_ref[...]
    s_old = s_ref[...]
    scores = _dot_nt(q, k) * mask_ref[...]
    o = _dot(scores.astype(BF16), v) + _dot((q.astype(F32) * qd_ref[...]).astype(BF16), s_old.astype(BF16))
    chunk_decay = jnp.exp(jnp.zeros((1, s_old.shape[1]), F32) + float(c) * lg)
    s_ref[...] = s_old * chunk_decay + _dot_tn((k.astype(F32) * kd_ref[...]).astype(BF16), v)
    if reverse:
        o = o + of_ref[...]
        o = o * lax.rsqrt(jnp.mean(o * o, axis=-1, keepdims=True) + EPS)
        o_ref[...] = (o * _silu(g_ref[...].astype(F32))).astype(o_ref.dtype)
    else:
        o_ref[...] = o


def _retention(qk, vg, lg_fwd, lg_bwd, n_lat):
    b, t, _ = qk.shape
    heads = qk.shape[2] // (2 * RET_DK)
    c = RET_CHUNK
    n_chunks, n_lat_chunks = t // c, n_lat // c

    def call(reverse, lg, extra_args, extra_specs, out_dtype):
        order = functools.partial(_chunk_order, n_lat_chunks=n_lat_chunks, n_chunks=n_chunks, reverse=reverse)
        in_specs = [pl.BlockSpec(memory_space=pltpu.SMEM),
                    pl.BlockSpec((None, c, RET_DK), lambda i, h, s: (i, order(s), h)),
                    pl.BlockSpec((None, c, RET_DK), lambda i, h, s: (i, order(s), heads + h)),
                    pl.BlockSpec((None, c, RET_DV), lambda i, h, s: (i, order(s), h))]
        in_specs += [spec(order) for spec in extra_specs]
        return pl.pallas_call(
            functools.partial(_ret_kernel, reverse=reverse),
            grid=(b, heads, n_chunks),
            in_specs=in_specs,
            out_specs=pl.BlockSpec((None, c, RET_DV), lambda i, h, s: (i, order(s), h)),
            out_shape=jax.ShapeDtypeStruct((b, t, heads * RET_DV), out_dtype),
            scratch_shapes=[pltpu.VMEM((RET_DK, RET_DV), F32), pltpu.VMEM((c, c), F32),
                            pltpu.VMEM((c, RET_DK), F32), pltpu.VMEM((c, RET_DK), F32)],
            compiler_params=_params("arbitrary", "arbitrary", "arbitrary"),
            name="retention_bwd" if reverse else "retention_fwd",
        )(lg, qk, qk, vg, *extra_args)

    o_f = call(False, lg_fwd, (), (), F32)
    of_spec = lambda order: pl.BlockSpec((None, c, RET_DV), lambda i, h, s: (i, order(s), h))
    g_spec = lambda order: pl.BlockSpec((None, c, RET_DV), lambda i, h, s: (i, order(s), heads + h))
    return call(True, lg_bwd, (o_f, vg), (of_spec, g_spec), BF16)


def _gla_kernel(q_ref, k_ref, lf_ref, v_ref, *rest, reverse, heads):
    if reverse:
        of_ref, g_ref, gain_ref, o_ref, st_ref = rest
    else:
        o_ref, st_ref = rest
    c = HGRN_CHUNK
    d = HGRN_DIM

    @pl.when(pl.program_id(1) == 0)
    def _():
        st_ref[...] = jnp.zeros(st_ref.shape, F32)

    i = lax.broadcasted_iota(jnp.int32, (c, c), 0)
    j = lax.broadcasted_iota(jnp.int32, (c, c), 1)
    tri = (j >= i) if reverse else (i >= j)
    tri_bf = jnp.where(tri, 1.0, 0.0).astype(BF16)
    lf = lf_ref[...]
    lf_hi = lf.astype(BF16)
    lf_lo = (lf - lf_hi.astype(F32)).astype(BF16)
    bcum = _dot(tri_bf, lf_hi) + _dot(tri_bf, lf_lo)
    b_last = bcum[0:1, :] if reverse else bcum[c - 1:c, :]
    q_in = q_ref[...].astype(F32) * jnp.exp(bcum)
    kf = k_ref[...].astype(F32)
    k_in = (kf * jnp.exp(-bcum)).astype(BF16)
    k_out = (kf * jnp.exp(b_last - bcum)).astype(BF16)
    decay = jnp.exp(b_last)
    q_in = q_in.astype(BF16)
    v = v_ref[...]
    for h in range(heads):
        sl = slice(h * d, (h + 1) * d)
        st = st_ref[h]
        scores = jnp.where(tri, _dot_nt(q_in[:, sl], k_in[:, sl]), 0.0)
        o = _dot_nt(q_in[:, sl], st.astype(BF16)) + _dot(scores.astype(BF16), v[:, sl])
        st_ref[h] = st * decay[:, sl] + _dot_tn(v[:, sl], k_out[:, sl])
        if reverse:
            o = o + of_ref[:, sl]
            o = o * lax.rsqrt(jnp.mean(o * o, axis=-1, keepdims=True) + EPS) * gain_ref[...]
            o_ref[:, sl] = (o * _silu(g_ref[:, sl].astype(F32))).astype(o_ref.dtype)
        else:
            o_ref[:, sl] = o


def _gla(qs, kf, lff, kb, lfb, ig, gain, n_lat, n_out_tiles=None):
    b, t, w = qs.shape
    heads = w // HGRN_DIM
    c = HGRN_CHUNK
    n_chunks, n_lat_chunks = t // c, n_lat // c

    def call(reverse, k, lf, extra_args, extra_specs, out_dtype):
        order = functools.partial(_chunk_order, n_lat_chunks=n_lat_chunks, n_chunks=n_chunks, reverse=reverse)
        blk = lambda col: pl.BlockSpec((None, c, w), lambda i, s: (i, order(s), col))
        return pl.pallas_call(
            functools.partial(_gla_kernel, reverse=reverse, heads=heads),
            grid=(b, n_chunks),
            in_specs=[blk(0), blk(0), blk(0), blk(0)] + [spec(blk) for spec in extra_specs],
            out_specs=blk(0),
            out_shape=jax.ShapeDtypeStruct((b, t, w), out_dtype),
            scratch_shapes=[pltpu.VMEM((heads, HGRN_DIM, HGRN_DIM), F32)],
            compiler_params=_params("arbitrary", "arbitrary"),
            name="gla_bwd" if reverse else "gla_fwd",
        )(qs, k, lf, ig, *extra_args)

    o_f = call(False, kf, lff, (), (), F32)
    gain_spec = lambda blk: pl.BlockSpec((1, HGRN_DIM), lambda i, s: (0, 0))
    return call(True, kb, lfb, (o_f, ig, gain.reshape(1, HGRN_DIM)),
                (lambda blk: blk(0), lambda blk: blk(1), gain_spec), BF16)


def _rope_tables(rot_dim, n_lat, n_ctx, batch, width):
    half = rot_dim // 2
    freqs = ROPE_THETA ** (-jnp.arange(0, half, 2, dtype=F32) / half)
    tok = jnp.arange(n_lat)
    a_row = (tok // GRID_W).astype(F32)[:, None] * freqs
    a_col = (tok % GRID_W).astype(F32)[:, None] * freqs
    cos = jnp.concatenate([jnp.cos(a_row), jnp.cos(a_row), jnp.cos(a_col), jnp.cos(a_col)], axis=-1)
    sin = jnp.concatenate([-jnp.sin(a_row), jnp.sin(a_row), -jnp.sin(a_col), jnp.sin(a_col)], axis=-1)
    cos = jnp.concatenate([cos, jnp.ones((n_ctx, rot_dim), F32)], axis=0)
    sin = jnp.concatenate([sin, jnp.zeros((n_ctx, rot_dim), F32)], axis=0)
    reps = (batch, width // rot_dim)
    return jnp.tile(cos, reps), jnp.tile(sin, reps)


MM_ROW_TILE = 1024
MM_COL_TILES = (1024, 512, 256, 128)


def _col_tile(*widths):
    return next(tn for tn in MM_COL_TILES if all(w % tn == 0 for w in widths))


def _retention_mixer(h2, w_in, lg_fwd, lg_bwd, b, t, n_lat):
    heads = lg_fwd.shape[0]
    hk = heads * RET_DK
    w = w_in.astype(BF16)
    cos, sin = _rope_tables(RET_DK, n_lat, t - n_lat, b, RET_DK)
    gain = jnp.concatenate([jnp.full((1, hk), RET_DK ** -0.5, F32), jnp.ones((1, hk), F32)], axis=1)
    tn = _col_tile(2 * hk, w.shape[1] - 2 * hk)
    qk, = _mm_call(functools.partial(_rope_body, swap=RET_DK // 4, norm=False), h2, w,
                   col_blk0=0, n_blk=2 * hk // tn, tn=tn, tm=MM_ROW_TILE, out_dtypes=[BF16],
                   row_ins=(cos, sin), col_ins=(gain,), name="ret_in_qk")
    vg, = _mm_call(functools.partial(_plain_body, scale=1.0), h2, w,
                   col_blk0=2 * hk // tn, n_blk=(w.shape[1] - 2 * hk) // tn, tn=tn, tm=MM_ROW_TILE,
                   out_dtypes=[BF16], name="ret_in_vg")
    return _retention(qk.reshape(b, t, -1), vg.reshape(b, t, -1), lg_fwd, lg_bwd, n_lat)


def _gqa_mixer(h2, w_in, q_gain, k_gain, b, t, n_lat):
    d = GQA_HEAD_DIM
    w = w_in.astype(BF16)
    n_qk = w.shape[1] * (GQA_GROUP + 1) // (GQA_GROUP + 2)
    hq = n_qk // d * GQA_GROUP // (GQA_GROUP + 1)
    cos, sin = _rope_tables(d, n_lat, t - n_lat, b, d)
    gain = jnp.concatenate([jnp.tile(q_gain * (d ** -0.5 * LOG2E), hq), jnp.tile(k_gain, n_qk // d - hq)])[None, :]
    tn = _col_tile(n_qk, w.shape[1] - n_qk)
    qk, = _mm_call(functools.partial(_rope_body, swap=d // 4, norm=True), h2, w,
                   col_blk0=0, n_blk=n_qk // tn, tn=tn, tm=MM_ROW_TILE, out_dtypes=[BF16],
                   row_ins=(cos, sin), col_ins=(gain.astype(F32),), name="gqa_in_qk")
    v, = _mm_call(functools.partial(_plain_body, scale=1.0), h2, w,
                  col_blk0=n_qk // tn, n_blk=(w.shape[1] - n_qk) // tn, tn=tn, tm=MM_ROW_TILE,
                  out_dtypes=[BF16], name="gqa_in_v")
    return _gqa_attention(qk.reshape(b, t, -1), v.reshape(b, t, -1), n_lat)


def _mla_mixer(h2, w_in, q_norm, w_q_up, kv_norm, w_kv_up, b, t, n_lat):
    qr, kvr = q_norm.shape[0], kv_norm.shape[0]
    heads = w_q_up.shape[1] // (MLA_NOPE + MLA_ROPE)
    rows = h2.shape[0]
    w = jnp.concatenate([w_in, w_in[:, qr + kvr:]], axis=1).astype(BF16)
    cos, sin = _rope_tables(MLA_ROPE, n_lat, t - n_lat, b, LANES)
    tm = MM_ROW_TILE
    row = lambda width: pl.BlockSpec((tm, width), lambda i: (i, 0))
    full = lambda r, c: pl.BlockSpec((r, c), lambda i: (0, 0))
    cq, ckv, kr = pl.pallas_call(
        functools.partial(_mla_in_body, qr=qr, kvr=kvr),
        grid=(rows // tm,),
        in_specs=[row(h2.shape[1]), full(*w.shape), row(LANES), row(LANES), full(1, qr), full(1, kvr)],
        out_specs=[row(qr), row(kvr), row(LANES)],
        out_shape=[jax.ShapeDtypeStruct((rows, qr), BF16), jax.ShapeDtypeStruct((rows, kvr), BF16),
                   jax.ShapeDtypeStruct((rows, LANES), BF16)],
        compiler_params=_params("arbitrary"),
        name="mla_in",
    )(h2, w, cos, sin, q_norm[None, :], kv_norm[None, :])
    wq = w_q_up.reshape(qr, heads, MLA_NOPE + MLA_ROPE)
    wq = jnp.concatenate([wq[:, :, :MLA_NOPE].reshape(qr, -1), wq[:, :, MLA_NOPE:].reshape(qr, -1)], axis=1).astype(BF16)
    wkv = w_kv_up.reshape(kvr, heads, MLA_NOPE + MLA_V)
    wkv = jnp.concatenate([wkv[:, :, :MLA_NOPE].reshape(kvr, -1), wkv[:, :, MLA_NOPE:].reshape(kvr, -1)], axis=1).astype(BF16)
    scale = (MLA_NOPE + MLA_ROPE) ** -0.5 * LOG2E
    n_nope = heads * MLA_NOPE
    n_rope = heads * MLA_ROPE
    tn = _col_tile(n_nope, n_rope)
    q_nope, = _mm_call(functools.partial(_plain_body, scale=scale), cq, wq, col_blk0=0, n_blk=n_nope // tn,
                       tn=tn, tm=tm, out_dtypes=[BF16], name="mla_q_nope")
    gain = jnp.full((1, n_rope), scale, F32)
    q_rope, = _mm_call(functools.partial(_rope_body, swap=MLA_ROPE // 4, norm=False), cq, wq,
                       col_blk0=n_nope // tn, n_blk=n_rope // tn, tn=tn, tm=tm, out_dtypes=[BF16],
                       row_ins=(cos, sin), col_ins=(gain,), name="mla_q_rope")
    q = jnp.concatenate([q_nope, q_rope], axis=1)
    tn = _col_tile(wkv.shape[1])
    kv, = _mm_call(functools.partial(_plain_body, scale=1.0), ckv, wkv, col_blk0=0, n_blk=wkv.shape[1] // tn,
                   tn=tn, tm=tm, out_dtypes=[BF16], name="mla_kv_up")
    return _mla_attention(q.reshape(b, t, -1), kv.reshape(b, t, -1), kr.reshape(b, t, -1), n_lat, heads)


def _hgrn_mixer(h2, w_in, lb_raw, out_gain, layer, b, t, n_lat):
    w = w_in.astype(BF16)
    width = w.shape[1] // 5
    tn, tm = _col_tile(width), MM_ROW_TILE
    nb = width // tn
    qs, = _mm_call(_silu_body, h2, w, col_blk0=0, n_blk=nb, tn=tn, tm=tm, out_dtypes=[BF16], name="hgrn_in_q")
    forget = functools.partial(_forget_body, layer=layer)
    kf, lff = _mm_call(forget, h2, w, col_blk0=nb, n_blk=nb, tn=tn, tm=tm, out_dtypes=[BF16, F32],
                       col_ins=(lb_raw,), name="hgrn_in_ff")
    kb, lfb = _mm_call(forget, h2, w, col_blk0=2 * nb, n_blk=nb, tn=tn, tm=tm, out_dtypes=[BF16, F32],
                       col_ins=(lb_raw,), name="hgrn_in_fb")
    ig, = _mm_call(functools.partial(_plain_body, scale=1.0), h2, w, col_blk0=3 * nb, n_blk=2 * nb, tn=tn, tm=tm,
                   out_dtypes=[BF16], name="hgrn_in_ig")
    r3 = lambda a: a.reshape(b, t, -1)
    return _gla(r3(qs), r3(kf), r3(lff), r3(kb), r3(lfb), r3(ig), out_gain, n_lat)


def kernel(x, c, ctx, c_ctx, ada_w, ada_b, ln_g, ln_b, ffn_w_in, ffn_w_out, ret_w_in, ret_decay_fwd, ret_decay_bwd, ret_w_out, gqa_w_in, gqa_q_norm, gqa_k_norm, gqa_w_out, mla_w_in, mla_q_norm, mla_w_q_up, mla_kv_norm, mla_w_kv_up, mla_w_out, hgrn_w_in, hgrn_lb_raw, hgrn_out_norm, hgrn_w_out):
    b, n_lat, d = x.shape
    n_ctx = ctx.shape[1]
    t = n_lat + n_ctx
    depth = ada_w.shape[0]
    alpha = (2 * depth) ** 0.25
    n_lat_tiles = n_lat // ROW_TILE
    n_tiles = t // ROW_TILE

    pad = (-(b + 1)) % 8
    cc = jnp.concatenate([c, c_ctx[None, :], jnp.zeros((pad, d), F32)], axis=0)
    mod = _ada_all(cc, ada_w, ada_b)

    def vec(layer, idx):
        m = mod[layer, :, idx * d:(idx + 1) * d]
        return jnp.stack([m[:b], jnp.broadcast_to(m[b], (b, d))], axis=1)[:, :, None, :]

    xs = jnp.concatenate([x, ctx], axis=1)
    h = _modulate0(xs, vec(0, 0), vec(0, 1), n_lat_tiles)

    for i in range(depth):
        m, j = i % N_MIXERS, i // N_MIXERS
        last = i == depth - 1
        h2 = h.reshape(b * t, d)
        if m == 0:
            a = _retention_mixer(h2, ret_w_in[j], ret_decay_fwd[j], ret_decay_bwd[j], b, t, n_lat)
            w_out = ret_w_out[j]
        elif m == 1:
            a = _gqa_mixer(h2, gqa_w_in[j], gqa_q_norm[j], gqa_k_norm[j], b, t, n_lat)
            w_out = gqa_w_out[j]
        elif m == 2:
            a = _mla_mixer(h2, mla_w_in[j], mla_q_norm[j], mla_w_q_up[j], mla_kv_norm[j], mla_w_kv_up[j], b, t, n_lat)
            w_out = mla_w_out[j]
        else:
            a = _hgrn_mixer(h2, hgrn_w_in[j], hgrn_lb_raw, hgrn_out_norm[j], i, b, t, n_lat)
            w_out = hgrn_w_out[j]
        xs, h = _outproj_ln(a, w_out.astype(BF16), xs, vec(i, 2), ln_g[i, 0][None, :], ln_b[i, 0][None, :],
                            vec(i, 3), vec(i, 4), alpha=alpha, n_lat_tiles=n_lat_tiles, n_tiles=n_tiles,
                            emit_h=True, name="mixer_out_ln")
        act = _ffn_in(h.reshape(b * t, d), ffn_w_in[i].astype(BF16), MM_ROW_TILE, 512).reshape(b, t, -1)
        if last:
            xs, = _outproj_ln(act, ffn_w_out[i].astype(BF16), xs, vec(i, 5), ln_g[i, 1][None, :], ln_b[i, 1][None, :],
                              vec(i, 5), vec(i, 5), alpha=alpha, n_lat_tiles=n_lat_tiles, n_tiles=n_lat_tiles,
                              emit_h=False, name="ffn_out_ln_final")
        else:
            xs, h = _outproj_ln(act, ffn_w_out[i].astype(BF16), xs, vec(i, 5), ln_g[i, 1][None, :], ln_b[i, 1][None, :],
                                vec(i + 1, 0), vec(i + 1, 1), alpha=alpha, n_lat_tiles=n_lat_tiles, n_tiles=n_tiles,
                                emit_h=True, name="ffn_out_ln")
    return xs
```

```python
import functools

import jax
import jax.numpy as jnp
from jax import lax
from jax.experimental import pallas as pl
from jax.experimental.pallas import tpu as pltpu

F32 = jnp.float32
BF16 = jnp.bfloat16

GRID_W = 64
N_MIXERS = 4
RET_DK = 256
RET_DV = 512
RET_CHUNK = 128
GQA_HEAD_DIM = 128
GQA_GROUP = 4
MLA_NOPE = 128
MLA_ROPE = 64
MLA_V = 128
HGRN_DIM = 128
HGRN_CHUNK = 64
ROPE_THETA = 10000.0
EPS = 1e-6

LANES = 128
ROW_TILE = 256
ATTN_KV_CHUNKS = (512, 256)
LOG2E = 1.4426950408889634
V7X_VMEM_LIMIT = 56 * 1024 * 1024


def _params(*sem):
    return pltpu.CompilerParams(dimension_semantics=sem, vmem_limit_bytes=V7X_VMEM_LIMIT)


def _dot(a, b):
    return jnp.dot(a, b, preferred_element_type=F32)


def _dot_nt(a, b):
    return lax.dot_general(a, b, (((1,), (1,)), ((), ())), preferred_element_type=F32)


def _dot_tn(a, b):
    return lax.dot_general(a, b, (((0,), (0,)), ((), ())), preferred_element_type=F32)


def _silu(x):
    return x * jax.nn.sigmoid(x)


def _swap_halves(x, w):
    if 2 * w == LANES:
        return pltpu.roll(x, w, axis=1)
    up = pltpu.roll(x, LANES - w, axis=1)
    down = pltpu.roll(x, w, axis=1)
    lane = lax.broadcasted_iota(jnp.int32, x.shape, 1)
    return jnp.where((lane % (2 * w)) < w, up, down)


def _ada_kernel(c_ref, w_ref, b_ref, o_ref):
    s = _silu(c_ref[...]).astype(BF16)
    o_ref[...] = _dot(s, w_ref[...].astype(BF16)) + b_ref[...]


def _ada_all(cc, ada_w, ada_b, tn=1024):
    depth, d, n = ada_w.shape
    rows = cc.shape[0]
    return pl.pallas_call(
        _ada_kernel,
        grid=(depth, n // tn),
        in_specs=[pl.BlockSpec((rows, d), lambda l, j: (0, 0)),
                  pl.BlockSpec((None, d, tn), lambda l, j: (l, 0, j)),
                  pl.BlockSpec((None, 1, tn), lambda l, j: (l, 0, j))],
        out_specs=pl.BlockSpec((None, rows, tn), lambda l, j: (l, 0, j)),
        out_shape=jax.ShapeDtypeStruct((depth, rows, n), F32),
        compiler_params=_params("arbitrary", "arbitrary"),
        name="ada_mod",
    )(cc, ada_w, ada_b.reshape(depth, 1, n))


def _mod_kernel(x_ref, sh_ref, sc_ref, h_ref):
    h_ref[...] = (x_ref[...] * (1.0 + sc_ref[...]) + sh_ref[...]).astype(BF16)


def _modulate0(x, sh, sc, n_lat_tiles):
    b, t, d = x.shape
    vec = pl.BlockSpec((None, None, 1, d), lambda i, j: (i, j // n_lat_tiles, 0, 0))
    return pl.pallas_call(
        _mod_kernel,
        grid=(b, t // ROW_TILE),
        in_specs=[pl.BlockSpec((None, ROW_TILE, d), lambda i, j: (i, j, 0)), vec, vec],
        out_specs=pl.BlockSpec((None, ROW_TILE, d), lambda i, j: (i, j, 0)),
        out_shape=jax.ShapeDtypeStruct((b, t, d), BF16),
        compiler_params=_params("arbitrary", "arbitrary"),
        name="modulate0",
    )(x, sh, sc)


def _with_bf16_weights(body, n_w):
    def kern(h_ref, *refs):
        w_refs, rest, wbf_refs = refs[:n_w], refs[n_w:len(refs) - n_w], refs[len(refs) - n_w:]

        @pl.when(pl.program_id(1) == 0)
        def _():
            for w_ref, wbf_ref in zip(w_refs, wbf_refs):
                wbf_ref[...] = w_ref[...].astype(BF16)

        body(h_ref, *wbf_refs, *rest)
    return kern


def _mm_call(body, h, w, *, col_blk0, n_blk, tn, tm, out_dtypes, name, row_ins=(), col_ins=(), const_ins=()):
    rows, k = h.shape
    assert rows % tm == 0 and w.shape[0] == k
    in_specs = [pl.BlockSpec((tm, k), lambda j, i: (i, 0)),
                pl.BlockSpec((k, tn), lambda j, i: (0, j + col_blk0))]
    args = [h, w]
    for a in row_ins:
        in_specs.append(pl.BlockSpec((tm, a.shape[1]), lambda j, i: (i, 0)))
        args.append(a)
    for a in col_ins:
        assert a.shape[1] == n_blk * tn
        in_specs.append(pl.BlockSpec((a.shape[0], tn), lambda j, i: (0, j)))
        args.append(a)
    for a in const_ins:
        in_specs.append(pl.BlockSpec(a.shape, lambda j, i: (0, 0)))
        args.append(a)
    scratch = []
    if w.dtype != BF16:
        body = _with_bf16_weights(body, 1)
        scratch = [pltpu.VMEM((k, tn), BF16)]
    out = pl.pallas_call(
        body,
        grid=(n_blk, rows // tm),
        in_specs=in_specs,
        out_specs=[pl.BlockSpec((tm, tn), lambda j, i: (i, j)) for _ in out_dtypes],
        out_shape=[jax.ShapeDtypeStruct((rows, n_blk * tn), dt) for dt in out_dtypes],
        scratch_shapes=scratch,
        compiler_params=_params("arbitrary", "arbitrary"),
        name=name,
    )(*args)
    return out


def _plain_body(h_ref, w_ref, o_ref, *, scale):
    acc = _dot(h_ref[...], w_ref[...])
    if scale != 1.0:
        acc = acc * scale
    o_ref[...] = acc.astype(o_ref.dtype)


def _silu_body(h_ref, w_ref, o_ref):
    o_ref[...] = _silu(_dot(h_ref[...], w_ref[...])).astype(o_ref.dtype)


def _rope_body(h_ref, w_ref, cos_ref, sin_ref, gain_ref, *rest, swap, norm):
    o_ref = rest[-1]
    acc = _dot(h_ref[...], w_ref[...])
    tn = acc.shape[1]
    tw = cos_ref.shape[1]
    if norm:
        acc = acc * lax.rsqrt(_dot((acc * acc).astype(BF16), rest[0][...]) + EPS)
    for c in range(tn // LANES):
        sl = slice(c * LANES, (c + 1) * LANES)
        x = acc[:, sl] * gain_ref[:, sl]
        t = c % (tw // LANES)
        tsl = slice(t * LANES, (t + 1) * LANES)
        y = x * cos_ref[:, tsl] + _swap_halves(x, swap) * sin_ref[:, tsl]
        o_ref[:, sl] = y.astype(o_ref.dtype)


def _forget_body(h_ref, w_ref, raw_ref, k_ref, lf_ref, *, layer):
    raw = raw_ref[...]
    e = jnp.exp(raw - jnp.max(raw, axis=0, keepdims=True))
    p = e / jnp.sum(e, axis=0, keepdims=True)
    lb = jnp.zeros_like(p[0:1])
    for r in range(1, layer + 1):
        lb = lb + p[r:r + 1]
    f = lb + (1.0 - lb) * jax.nn.sigmoid(_dot(h_ref[...], w_ref[...]))
    k_ref[...] = (1.0 - f).astype(k_ref.dtype)
    lf_ref[...] = jnp.log(f)


def _mla_in_body(h_ref, w_ref, cos_ref, sin_ref, qn_ref, kvn_ref, cq_ref, ckv_ref, kr_ref, *, qr, kvr):
    acc = _dot(h_ref[...], w_ref[...])
    cq = acc[:, :qr]
    cq_ref[...] = (cq * lax.rsqrt(jnp.mean(cq * cq, axis=-1, keepdims=True) + EPS) * qn_ref[...]).astype(BF16)
    ckv = acc[:, qr:qr + kvr]
    ckv_ref[...] = (ckv * lax.rsqrt(jnp.mean(ckv * ckv, axis=-1, keepdims=True) + EPS) * kvn_ref[...]).astype(BF16)
    kr = acc[:, qr + kvr:]
    kr_ref[...] = (kr * cos_ref[...] + _swap_halves(kr, MLA_ROPE // 4) * sin_ref[...]).astype(BF16)


def _swiglu_body(h_ref, wa_ref, wb_ref, o_ref):
    h = h_ref[...]
    a = _dot(h, wa_ref[...])
    b = _dot(h, wb_ref[...])
    o_ref[...] = (_silu(a) * b).astype(o_ref.dtype)


def _ffn_in(h, w_all, layer, tm, tn):
    rows, k = h.shape
    hidden = w_all.shape[2] // 2
    nb = hidden // tn
    w = w_all
    return pl.pallas_call(
        _with_bf16_weights(_swiglu_body, 2),
        grid=(nb, rows // tm),
        in_specs=[pl.BlockSpec((tm, k), lambda j, i: (i, 0)),
                  pl.BlockSpec((None, k, tn), lambda j, i: (layer, 0, j)),
                  pl.BlockSpec((None, k, tn), lambda j, i: (layer, 0, j + nb))],
        out_specs=pl.BlockSpec((tm, tn), lambda j, i: (i, j)),
        out_shape=jax.ShapeDtypeStruct((rows, hidden), BF16),
        scratch_shapes=[pltpu.VMEM((k, tn), BF16), pltpu.VMEM((k, tn), BF16)],
        compiler_params=_params("arbitrary", "arbitrary"),
        name="ffn_in_swiglu",
    )(h, w, w)


def _outproj_ln_kernel(a_ref, w_ref, x_ref, gate_ref, lng_ref, lnb_ref, *rest, alpha):
    y = _dot(a_ref[...], w_ref[...])
    z = alpha * x_ref[...] + gate_ref[...] * y
    zc = z - jnp.mean(z, axis=-1, keepdims=True)
    var = jnp.mean(zc * zc, axis=-1, keepdims=True)
    xn = zc * lax.rsqrt(var + EPS) * lng_ref[...] + lnb_ref[...]
    if len(rest) == 1:
        rest[0][...] = xn
    else:
        sh_ref, sc_ref, xo_ref, ho_ref = rest
        xo_ref[...] = xn
        ho_ref[...] = (xn * (1.0 + sc_ref[...]) + sh_ref[...]).astype(BF16)


def _outproj_ln(a, w, x, gate, lng, lnb, next_mod, *, alpha, n_lat_tiles, n_tiles, name):
    b, _, k = a.shape
    d = w.shape[1]
    tile = lambda width: pl.BlockSpec((None, ROW_TILE, width), lambda i, j: (i, j, 0))
    vec = pl.BlockSpec((None, None, 1, d), lambda i, j: (i, j // n_lat_tiles, 0, 0))
    full = pl.BlockSpec((1, d), lambda i, j: (0, 0))
    out_shape = [jax.ShapeDtypeStruct((b, n_tiles * ROW_TILE, d), F32)]
    if next_mod:
        out_shape.append(jax.ShapeDtypeStruct((b, n_tiles * ROW_TILE, d), BF16))
    return pl.pallas_call(
        functools.partial(_outproj_ln_kernel, alpha=alpha),
        grid=(b, n_tiles),
        in_specs=[tile(k),
                  pl.BlockSpec((k, d), lambda i, j: (0, 0), pipeline_mode=pl.Buffered(1)),
                  tile(d), vec, full, full] + [vec for _ in next_mod],
        out_specs=[tile(d) for _ in out_shape],
        out_shape=out_shape,
        compiler_params=_params("arbitrary", "arbitrary"),
        name=name,
    )(a, w, x, gate, lng, lnb, *next_mod)


def _kv_chunk(n_lat):
    return next(ck for ck in ATTN_KV_CHUNKS if n_lat % ck == 0)


def _fill_vt(v, vt_ref, vtc_ref, n_lat, ck):
    dv = v.shape[1]
    vt = v.astype(F32).T.astype(BF16)
    for c in range(n_lat // ck):
        vt_ref[c, 0:dv, :] = vt[:, c * ck:(c + 1) * ck]
        vt_ref[c, dv:, :] = jnp.ones((vt_ref.shape[1] - dv, ck), BF16)
    vtc_ref[0:dv, :] = vt[:, n_lat:]
    vtc_ref[dv:, :] = jnp.ones((vtc_ref.shape[0] - dv, vtc_ref.shape[1]), BF16)


def _attend(streams, is_ctx, n_chunks, m_ref, acc_ref, sa_ref, sb_ref):
    assert n_chunks % 2 == 0
    n = len(streams)
    dv = acc_ref.shape[1] // 2
    m_ref[...] = jnp.full(m_ref.shape, -jnp.inf, F32)
    acc_ref[...] = jnp.zeros(acc_ref.shape, F32)

    def accumulate(h, s, vtc):
        m_old = m_ref[h]
        m_new = jnp.maximum(m_old, jnp.max(s, axis=0, keepdims=True))
        p = jnp.exp2(s - m_new).astype(BF16)
        acc_ref[h] = jnp.exp2(m_old - m_new) * acc_ref[h] + _dot(vtc, p)
        m_ref[h] = m_new

    def stage(c_next, buf_next, c_cur, buf_cur):
        for h, (qs, k_lat, _, vt_ref, _) in enumerate(streams):
            if c_next is not None:
                buf_next[h] = _dot_nt(k_lat(c_next), qs)
            if c_cur is not None:
                accumulate(h, buf_cur[h], vt_ref[c_cur])

    @pl.when(jnp.logical_not(is_ctx))
    def _():
        stage(0, sa_ref, None, None)

        def pair(i, carry):
            stage(2 * i + 1, sb_ref, 2 * i, sa_ref)
            stage(2 * i + 2, sa_ref, 2 * i + 1, sb_ref)
            return carry

        lax.fori_loop(0, n_chunks // 2 - 1, pair, 0)
        stage(n_chunks - 1, sb_ref, n_chunks - 2, sa_ref)
        stage(None, None, n_chunks - 1, sb_ref)

    outs = []
    for h, (qs, _, k_ctx, _, vtc_ref) in enumerate(streams):
        accumulate(h, _dot_nt(k_ctx, qs), vtc_ref[...])
        acc = acc_ref[h]
        outs.append(acc[:dv] / acc[dv:])
    return outs


def _gqa_attn_kernel(q_ref, k_ref, v_ref, o_ref, vt_ref, vtc_ref, m_ref, acc_ref, sa_ref, sb_ref, *, n_lat, ck):
    d = GQA_HEAD_DIM
    tq = q_ref.shape[0]

    @pl.when(pl.program_id(2) == 0)
    def _():
        _fill_vt(v_ref[...], vt_ref, vtc_ref, n_lat, ck)

    is_ctx = pl.program_id(2) >= n_lat // tq
    q = q_ref[...]
    qs = jnp.concatenate([q[:, g * d:(g + 1) * d] for g in range(GQA_GROUP)], axis=0)
    k_lat = lambda c: k_ref[pl.ds(pl.multiple_of(c * ck, ck), ck), :]
    o_t, = _attend([(qs, k_lat, k_ref[n_lat:, :], vt_ref, vtc_ref)], is_ctx, n_lat // ck,
                   m_ref, acc_ref, sa_ref, sb_ref)
    for g in range(GQA_GROUP):
        o_ref[:, g * d:(g + 1) * d] = o_t[:, g * tq:(g + 1) * tq].T.astype(o_ref.dtype)


def _gqa_attention(qk, v, n_lat):
    b, t, _ = qk.shape
    d = GQA_HEAD_DIM
    hkv = v.shape[2] // d
    hq = hkv * GQA_GROUP
    gw = GQA_GROUP * d
    ck = _kv_chunk(n_lat)
    rows = GQA_GROUP * ROW_TILE
    return pl.pallas_call(
        functools.partial(_gqa_attn_kernel, n_lat=n_lat, ck=ck),
        grid=(b, hkv, t // ROW_TILE),
        in_specs=[pl.BlockSpec((None, ROW_TILE, gw), lambda i, h, j: (i, j, h)),
                  pl.BlockSpec((None, t, d), lambda i, h, j: (i, 0, hq + h)),
                  pl.BlockSpec((None, t, d), lambda i, h, j: (i, 0, h))],
        out_specs=pl.BlockSpec((None, ROW_TILE, gw), lambda i, h, j: (i, j, h)),
        out_shape=jax.ShapeDtypeStruct((b, t, hq * d), BF16),
        scratch_shapes=[pltpu.VMEM((n_lat // ck, 2 * d, ck), BF16), pltpu.VMEM((2 * d, t - n_lat), BF16),
                        pltpu.VMEM((1, 1, rows), F32), pltpu.VMEM((1, 2 * d, rows), F32),
                        pltpu.VMEM((1, ck, rows), F32), pltpu.VMEM((1, ck, rows), F32)],
        compiler_params=_params("arbitrary", "arbitrary", "arbitrary"),
        name="gqa_attention",
    )(qk, qk, v)


MLA_HEAD_GROUP = 4


def _mla_attn_kernel(qn_ref, qr_ref, kn_ref, kr_ref, v_ref, o_ref, vt_ref, vtc_ref, m_ref, acc_ref, sa_ref, sb_ref, *,
                     n_lat, ck):
    tq = qn_ref.shape[0]
    n_chunks = n_lat // ck

    @pl.when(pl.program_id(2) == 0)
    def _():
        for hh in range(MLA_HEAD_GROUP):
            _fill_vt(v_ref[:, hh * MLA_V:(hh + 1) * MLA_V], vt_ref.at[pl.ds(hh * n_chunks, n_chunks)],
                     vtc_ref.at[hh], n_lat, ck)

    is_ctx = pl.program_id(2) >= n_lat // tq
    lane = lax.broadcasted_iota(jnp.int32, (tq, LANES), 1)
    streams = []
    for hh in range(MLA_HEAD_GROUP):
        nope = slice(hh * MLA_NOPE, (hh + 1) * MLA_NOPE)
        pair = qr_ref[:, (hh // 2) * LANES:(hh // 2 + 1) * LANES]
        qr = jnp.where((lane // MLA_ROPE) == (hh % 2), pair, jnp.zeros_like(pair))
        qs = jnp.concatenate([qn_ref[:, nope], qr], axis=1)

        def k_lat(c, nope=nope):
            sl = pl.ds(pl.multiple_of(c * ck, ck), ck)
            return jnp.concatenate([kn_ref[sl, nope], kr_ref[sl, :]], axis=1)

        k_ctx = jnp.concatenate([kn_ref[n_lat:, nope], kr_ref[n_lat:, :]], axis=1)
        streams.append((qs, k_lat, k_ctx, vt_ref.at[pl.ds(hh * n_chunks, n_chunks)], vtc_ref.at[hh]))
    outs = _attend(streams, is_ctx, n_chunks, m_ref, acc_ref, sa_ref, sb_ref)
    for hh, o_t in enumerate(outs):
        o_ref[:, hh * MLA_V:(hh + 1) * MLA_V] = o_t.T.astype(o_ref.dtype)


def _mla_attention(q, kv, kr, n_lat, heads):
    b, t, _ = q.shape
    hg = MLA_HEAD_GROUP
    ck = _kv_chunk(n_lat)
    n_groups = heads // hg
    rope_w = hg * MLA_ROPE
    return pl.pallas_call(
        functools.partial(_mla_attn_kernel, n_lat=n_lat, ck=ck),
        grid=(b, n_groups, t // ROW_TILE),
        in_specs=[pl.BlockSpec((None, ROW_TILE, hg * MLA_NOPE), lambda i, h, j: (i, j, h)),
                  pl.BlockSpec((None, ROW_TILE, rope_w), lambda i, h, j: (i, j, heads * MLA_NOPE // rope_w + h)),
                  pl.BlockSpec((None, t, hg * MLA_NOPE), lambda i, h, j: (i, 0, h)),
                  pl.BlockSpec((None, t, LANES), lambda i, h, j: (i, 0, 0)),
                  pl.BlockSpec((None, t, hg * MLA_V), lambda i, h, j: (i, 0, n_groups + h))],
        out_specs=pl.BlockSpec((None, ROW_TILE, hg * MLA_V), lambda i, h, j: (i, j, h)),
        out_shape=jax.ShapeDtypeStruct((b, t, heads * MLA_V), BF16),
        scratch_shapes=[pltpu.VMEM((hg * (n_lat // ck), 2 * MLA_V, ck), BF16),
                        pltpu.VMEM((hg, 2 * MLA_V, t - n_lat), BF16),
                        pltpu.VMEM((hg, 1, ROW_TILE), F32), pltpu.VMEM((hg, 2 * MLA_V, ROW_TILE), F32),
                        pltpu.VMEM((hg, ck, ROW_TILE), F32), pltpu.VMEM((hg, ck, ROW_TILE), F32)],
        compiler_params=_params("arbitrary", "arbitrary", "arbitrary"),
        name="mla_attention",
    )(q, q, kv, kr, kv)


def _chunk_order(step, n_lat_chunks, n_chunks, reverse):
    if reverse:
        return n_chunks - 1 - step
    n_ctx = n_chunks - n_lat_chunks
    return jnp.where(step < n_ctx, n_lat_chunks + step, step - n_ctx)


def _ret_kernel(lg_ref, q_ref, k_ref, v_ref, *rest, reverse, heads):
    if reverse:
        of_ref, g_ref, o_ref, s_ref, mask_ref, qd_ref, kd_ref = rest
    else:
        o_ref, s_ref, mask_ref, qd_ref, kd_ref = rest
    c = RET_CHUNK

    @pl.when(pl.program_id(1) == 0)
    def _():
        s_ref[...] = jnp.zeros(s_ref.shape, F32)
        i = lax.broadcasted_iota(jnp.int32, (c, c), 0)
        j = lax.broadcasted_iota(jnp.int32, (c, c), 1)
        diff = (j - i) if reverse else (i - j)
        pos = lax.broadcasted_iota(jnp.int32, (c, RET_DK), 0)
        pos = (c - 1 - pos) if reverse else pos
        for h in range(heads):
            lg = lg_ref[h]
            mask_ref[h] = jnp.where(diff >= 0, jnp.exp(jnp.maximum(diff, 0).astype(F32) * lg), 0.0)
            qd_ref[h] = jnp.exp((pos + 1).astype(F32) * lg)
            kd_ref[h] = jnp.exp((c - 1 - pos).astype(F32) * lg)

    for h in range(heads):
        ksl = slice(h * RET_DK, (h + 1) * RET_DK)
        vsl = slice(h * RET_DV, (h + 1) * RET_DV)
        q = q_ref[:, ksl]
        k = k_ref[:, ksl]
        v = v_ref[:, vsl]
        s_old = s_ref[h]
        scores = _dot_nt(q, k) * mask_ref[h]
        o = _dot(scores.astype(BF16), v) + _dot((q.astype(F32) * qd_ref[h]).astype(BF16), s_old.astype(BF16))
        chunk_decay = jnp.exp(jnp.zeros((1, RET_DV), F32) + float(c) * lg_ref[h])
        s_ref[h] = s_old * chunk_decay + _dot_tn((k.astype(F32) * kd_ref[h]).astype(BF16), v)
        if reverse:
            o = o + of_ref[:, vsl]
            o = o * lax.rsqrt(jnp.mean(o * o, axis=-1, keepdims=True) + EPS)
            o_ref[:, vsl] = (o * _silu(g_ref[:, vsl].astype(F32))).astype(o_ref.dtype)
        else:
            o_ref[:, vsl] = o


def _retention(qk, vg, lg_fwd, lg_bwd, n_lat):
    b, t, _ = qk.shape
    heads = qk.shape[2] // (2 * RET_DK)
    c = RET_CHUNK
    n_chunks, n_lat_chunks = t // c, n_lat // c
    wk, wv = heads * RET_DK, heads * RET_DV

    def call(reverse, lg, extra_args, extra_specs, out_dtype):
        order = functools.partial(_chunk_order, n_lat_chunks=n_lat_chunks, n_chunks=n_chunks, reverse=reverse)
        blk = lambda width, col: pl.BlockSpec((None, c, width), lambda i, s: (i, order(s), col))
        return pl.pallas_call(
            functools.partial(_ret_kernel, reverse=reverse, heads=heads),
            grid=(b, n_chunks),
            in_specs=[pl.BlockSpec(memory_space=pltpu.SMEM), blk(wk, 0), blk(wk, 1), blk(wv, 0)]
                     + [blk(wv, col) for col in extra_specs],
            out_specs=blk(wv, 0),
            out_shape=jax.ShapeDtypeStruct((b, t, wv), out_dtype),
            scratch_shapes=[pltpu.VMEM((heads, RET_DK, RET_DV), F32), pltpu.VMEM((heads, c, c), F32),
                            pltpu.VMEM((heads, c, RET_DK), F32), pltpu.VMEM((heads, c, RET_DK), F32)],
            compiler_params=_params("arbitrary", "arbitrary"),
            name="retention_bwd" if reverse else "retention_fwd",
        )(lg, qk, qk, vg, *extra_args)

    o_f = call(False, lg_fwd, (), (), F32)
    return call(True, lg_bwd, (o_f, vg), (0, 1), BF16)


def _gla_kernel(q_ref, k_ref, lf_ref, v_ref, *rest, reverse, heads):
    if reverse:
        of_ref, g_ref, gain_ref, o_ref, st_ref = rest
    else:
        o_ref, st_ref = rest
    c = HGRN_CHUNK
    d = HGRN_DIM

    @pl.when(pl.program_id(1) == 0)
    def _():
        st_ref[...] = jnp.zeros(st_ref.shape, F32)

    i = lax.broadcasted_iota(jnp.int32, (c, c), 0)
    j = lax.broadcasted_iota(jnp.int32, (c, c), 1)
    tri = (j >= i) if reverse else (i >= j)
    tri_bf = jnp.where(tri, 1.0, 0.0).astype(BF16)
    lf = lf_ref[...]
    lf_hi = lf.astype(BF16)
    lf_lo = (lf - lf_hi.astype(F32)).astype(BF16)
    bcum = _dot(tri_bf, lf_hi) + _dot(tri_bf, lf_lo)
    b_last = bcum[0:1, :] if reverse else bcum[c - 1:c, :]
    q_in = q_ref[...].astype(F32) * jnp.exp(bcum)
    kf = k_ref[...].astype(F32)
    k_in = (kf * jnp.exp(-bcum)).astype(BF16)
    k_out = (kf * jnp.exp(b_last - bcum)).astype(BF16)
    decay = jnp.exp(b_last)
    q_in = q_in.astype(BF16)
    v = v_ref[...]
    for h in range(heads):
        sl = slice(h * d, (h + 1) * d)
        st = st_ref[h]
        scores = jnp.where(tri, _dot_nt(q_in[:, sl], k_in[:, sl]), 0.0)
        o = _dot_nt(q_in[:, sl], st.astype(BF16)) + _dot(scores.astype(BF16), v[:, sl])
        st_ref[h] = st * decay[:, sl] + _dot_tn(v[:, sl], k_out[:, sl])
        if reverse:
            o = o + of_ref[:, sl]
            o = o * lax.rsqrt(jnp.mean(o * o, axis=-1, keepdims=True) + EPS) * gain_ref[...]
            o_ref[:, sl] = (o * _silu(g_ref[:, sl].astype(F32))).astype(o_ref.dtype)
        else:
            o_ref[:, sl] = o


def _gla(qs, kf, lff, kb, lfb, ig, gain, n_lat, n_out_tiles=None):
    b, t, w = qs.shape
    heads = w // HGRN_DIM
    c = HGRN_CHUNK
    n_chunks, n_lat_chunks = t // c, n_lat // c

    def call(reverse, k, lf, extra_args, extra_specs, out_dtype):
        order = functools.partial(_chunk_order, n_lat_chunks=n_lat_chunks, n_chunks=n_chunks, reverse=reverse)
        blk = lambda col: pl.BlockSpec((None, c, w), lambda i, s: (i, order(s), col))
        return pl.pallas_call(
            functools.partial(_gla_kernel, reverse=reverse, heads=heads),
            grid=(b, n_chunks),
            in_specs=[blk(0), blk(0), blk(0), blk(0)] + [spec(blk) for spec in extra_specs],
            out_specs=blk(0),
            out_shape=jax.ShapeDtypeStruct((b, t, w), out_dtype),
            scratch_shapes=[pltpu.VMEM((heads, HGRN_DIM, HGRN_DIM), F32)],
            compiler_params=_params("arbitrary", "arbitrary"),
            name="gla_bwd" if reverse else "gla_fwd",
        )(qs, k, lf, ig, *extra_args)

    o_f = call(False, kf, lff, (), (), F32)
    gain_spec = lambda blk: pl.BlockSpec((1, HGRN_DIM), lambda i, s: (0, 0))
    return call(True, kb, lfb, (o_f, ig, gain.reshape(1, HGRN_DIM)),
                (lambda blk: blk(0), lambda blk: blk(1), gain_spec), BF16)


def _rope_tables(rot_dim, n_lat, n_ctx, batch, width):
    half = rot_dim // 2
    freqs = ROPE_THETA ** (-jnp.arange(0, half, 2, dtype=F32) / half)
    tok = jnp.arange(n_lat)
    a_row = (tok // GRID_W).astype(F32)[:, None] * freqs
    a_col = (tok % GRID_W).astype(F32)[:, None] * freqs
    cos = jnp.concatenate([jnp.cos(a_row), jnp.cos(a_row), jnp.cos(a_col), jnp.cos(a_col)], axis=-1)
    sin = jnp.concatenate([-jnp.sin(a_row), jnp.sin(a_row), -jnp.sin(a_col), jnp.sin(a_col)], axis=-1)
    cos = jnp.concatenate([cos, jnp.ones((n_ctx, rot_dim), F32)], axis=0)
    sin = jnp.concatenate([sin, jnp.zeros((n_ctx, rot_dim), F32)], axis=0)
    reps = (batch, width // rot_dim)
    return jnp.tile(cos, reps), jnp.tile(sin, reps)


MM_ROW_TILE = 1024
MM_COL_TILES = (1024, 512, 256, 128)


def _col_tile(*widths):
    return next(tn for tn in MM_COL_TILES if all(w % tn == 0 for w in widths))


def _retention_mixer(h2, w_in, lg_fwd, lg_bwd, b, t, n_lat):
    heads = lg_fwd.shape[0]
    hk = heads * RET_DK
    w = w_in
    cos, sin = _rope_tables(RET_DK, n_lat, t - n_lat, b, RET_DK)
    gain = jnp.concatenate([jnp.full((1, hk), RET_DK ** -0.5, F32), jnp.ones((1, hk), F32)], axis=1)
    tn = _col_tile(2 * hk, w.shape[1] - 2 * hk)
    qk, = _mm_call(functools.partial(_rope_body, swap=RET_DK // 4, norm=False), h2, w,
                   col_blk0=0, n_blk=2 * hk // tn, tn=tn, tm=MM_ROW_TILE, out_dtypes=[BF16],
                   row_ins=(cos, sin), col_ins=(gain,), name="ret_in_qk")
    vg, = _mm_call(functools.partial(_plain_body, scale=1.0), h2, w,
                   col_blk0=2 * hk // tn, n_blk=(w.shape[1] - 2 * hk) // tn, tn=tn, tm=MM_ROW_TILE,
                   out_dtypes=[BF16], name="ret_in_vg")
    return _retention(qk.reshape(b, t, -1), vg.reshape(b, t, -1), lg_fwd, lg_bwd, n_lat)


def _gqa_mixer(h2, w_in, q_gain, k_gain, b, t, n_lat):
    d = GQA_HEAD_DIM
    w = w_in
    n_qk = w.shape[1] * (GQA_GROUP + 1) // (GQA_GROUP + 2)
    hq = n_qk // d * GQA_GROUP // (GQA_GROUP + 1)
    cos, sin = _rope_tables(d, n_lat, t - n_lat, b, d)
    gain = jnp.concatenate([jnp.tile(q_gain * (d ** -0.5 * LOG2E), hq), jnp.tile(k_gain, n_qk // d - hq)])[None, :]
    tn = _col_tile(n_qk, w.shape[1] - n_qk)
    head_of = jnp.arange(tn) // d
    block_mean = jnp.where(head_of[:, None] == head_of[None, :], 1.0 / d, 0.0).astype(BF16)
    qk, = _mm_call(functools.partial(_rope_body, swap=d // 4, norm=True), h2, w,
                   col_blk0=0, n_blk=n_qk // tn, tn=tn, tm=MM_ROW_TILE, out_dtypes=[BF16],
                   row_ins=(cos, sin), col_ins=(gain.astype(F32),), const_ins=(block_mean,), name="gqa_in_qk")
    v, = _mm_call(functools.partial(_plain_body, scale=1.0), h2, w,
                  col_blk0=n_qk // tn, n_blk=(w.shape[1] - n_qk) // tn, tn=tn, tm=MM_ROW_TILE,
                  out_dtypes=[BF16], name="gqa_in_v")
    return _gqa_attention(qk.reshape(b, t, -1), v.reshape(b, t, -1), n_lat)


def _mla_mixer(h2, w_in, q_norm, w_q_up, kv_norm, w_kv_up, b, t, n_lat):
    qr, kvr = q_norm.shape[0], kv_norm.shape[0]
    heads = w_q_up.shape[1] // (MLA_NOPE + MLA_ROPE)
    rows = h2.shape[0]
    w = jnp.concatenate([w_in, w_in[:, qr + kvr:]], axis=1).astype(BF16)
    cos, sin = _rope_tables(MLA_ROPE, n_lat, t - n_lat, b, LANES)
    tm = MM_ROW_TILE
    row = lambda width: pl.BlockSpec((tm, width), lambda i: (i, 0))
    full = lambda r, c: pl.BlockSpec((r, c), lambda i: (0, 0))
    cq, ckv, kr = pl.pallas_call(
        functools.partial(_mla_in_body, qr=qr, kvr=kvr),
        grid=(rows // tm,),
        in_specs=[row(h2.shape[1]), full(*w.shape), row(LANES), row(LANES), full(1, qr), full(1, kvr)],
        out_specs=[row(qr), row(kvr), row(LANES)],
        out_shape=[jax.ShapeDtypeStruct((rows, qr), BF16), jax.ShapeDtypeStruct((rows, kvr), BF16),
                   jax.ShapeDtypeStruct((rows, LANES), BF16)],
        compiler_params=_params("arbitrary"),
        name="mla_in",
    )(h2, w, cos, sin, q_norm[None, :], kv_norm[None, :])
    wq = w_q_up.reshape(qr, heads, MLA_NOPE + MLA_ROPE)
    wq = jnp.concatenate([wq[:, :, :MLA_NOPE].reshape(qr, -1), wq[:, :, MLA_NOPE:].reshape(qr, -1)], axis=1).astype(BF16)
    wkv = w_kv_up.reshape(kvr, heads, MLA_NOPE + MLA_V)
    wkv = jnp.concatenate([wkv[:, :, :MLA_NOPE].reshape(kvr, -1), wkv[:, :, MLA_NOPE:].reshape(kvr, -1)], axis=1).astype(BF16)
    scale = (MLA_NOPE + MLA_ROPE) ** -0.5 * LOG2E
    n_nope = heads * MLA_NOPE
    n_rope = heads * MLA_ROPE
    tn = _col_tile(n_nope, n_rope)
    q_nope, = _mm_call(functools.partial(_plain_body, scale=scale), cq, wq, col_blk0=0, n_blk=n_nope // tn,
                       tn=tn, tm=tm, out_dtypes=[BF16], name="mla_q_nope")
    gain = jnp.full((1, n_rope), scale, F32)
    q_rope, = _mm_call(functools.partial(_rope_body, swap=MLA_ROPE // 4, norm=False), cq, wq,
                       col_blk0=n_nope // tn, n_blk=n_rope // tn, tn=tn, tm=tm, out_dtypes=[BF16],
                       row_ins=(cos, sin), col_ins=(gain,), name="mla_q_rope")
    q = jnp.concatenate([q_nope, q_rope], axis=1)
    tn = _col_tile(wkv.shape[1])
    kv, = _mm_call(functools.partial(_plain_body, scale=1.0), ckv, wkv, col_blk0=0, n_blk=wkv.shape[1] // tn,
                   tn=tn, tm=tm, out_dtypes=[BF16], name="mla_kv_up")
    return _mla_attention(q.reshape(b, t, -1), kv.reshape(b, t, -1), kr.reshape(b, t, -1), n_lat, heads)


def _hgrn_mixer(h2, w_in, lb_raw, out_gain, layer, b, t, n_lat):
    w = w_in
    width = w.shape[1] // 5
    tn, tm = _col_tile(width), MM_ROW_TILE
    nb = width // tn
    qs, = _mm_call(_silu_body, h2, w, col_blk0=0, n_blk=nb, tn=tn, tm=tm, out_dtypes=[BF16], name="hgrn_in_q")
    forget = functools.partial(_forget_body, layer=layer)
    kf, lff = _mm_call(forget, h2, w, col_blk0=nb, n_blk=nb, tn=tn, tm=tm, out_dtypes=[BF16, F32],
                       col_ins=(lb_raw,), name="hgrn_in_ff")
    kb, lfb = _mm_call(forget, h2, w, col_blk0=2 * nb, n_blk=nb, tn=tn, tm=tm, out_dtypes=[BF16, F32],
                       col_ins=(lb_raw,), name="hgrn_in_fb")
    ig, = _mm_call(functools.partial(_plain_body, scale=1.0), h2, w, col_blk0=3 * nb, n_blk=2 * nb, tn=tn, tm=tm,
                   out_dtypes=[BF16], name="hgrn_in_ig")
    r3 = lambda a: a.reshape(b, t, -1)
    return _gla(r3(qs), r3(kf), r3(lff), r3(kb), r3(lfb), r3(ig), out_gain, n_lat)


def kernel(x, c, ctx, c_ctx, ada_w, ada_b, ln_g, ln_b, ffn_w_in, ffn_w_out, ret_w_in, ret_decay_fwd, ret_decay_bwd, ret_w_out, gqa_w_in, gqa_q_norm, gqa_k_norm, gqa_w_out, mla_w_in, mla_q_norm, mla_w_q_up, mla_kv_norm, mla_w_kv_up, mla_w_out, hgrn_w_in, hgrn_lb_raw, hgrn_out_norm, hgrn_w_out):
    b, n_lat, d = x.shape
    n_ctx = ctx.shape[1]
    t = n_lat + n_ctx
    depth = ada_w.shape[0]
    alpha = (2 * depth) ** 0.25
    n_lat_tiles = n_lat // ROW_TILE
    n_tiles = t // ROW_TILE

    pad = (-(b + 1)) % 8
    cc = jnp.concatenate([c, c_ctx[None, :], jnp.zeros((pad, d), F32)], axis=0)
    mod = _ada_all(cc, ada_w, ada_b)

    def vec(layer, idx):
        m = mod[layer, :, idx * d:(idx + 1) * d]
        return jnp.stack([m[:b], jnp.broadcast_to(m[b], (b, d))], axis=1)[:, :, None, :]

    xs = jnp.concatenate([x, ctx], axis=1)
    h = _modulate0(xs, vec(0, 0), vec(0, 1), n_lat_tiles)

    for i in range(depth):
        m, j = i % N_MIXERS, i // N_MIXERS
        last = i == depth - 1
        h2 = h.reshape(b * t, d)
        if m == 0:
            a = _retention_mixer(h2, ret_w_in[j], ret_decay_fwd[j], ret_decay_bwd[j], b, t, n_lat)
            w_out = ret_w_out[j]
        elif m == 1:
            a = _gqa_mixer(h2, gqa_w_in[j], gqa_q_norm[j], gqa_k_norm[j], b, t, n_lat)
            w_out = gqa_w_out[j]
        elif m == 2:
            a = _mla_mixer(h2, mla_w_in[j], mla_q_norm[j], mla_w_q_up[j], mla_kv_norm[j], mla_w_kv_up[j], b, t, n_lat)
            w_out = mla_w_out[j]
        else:
            a = _hgrn_mixer(h2, hgrn_w_in[j], hgrn_lb_raw, hgrn_out_norm[j], i, b, t, n_lat)
            w_out = hgrn_w_out[j]
        xs, h = _outproj_ln(a, w_out.astype(BF16), xs, vec(i, 2), ln_g[i, 0][None, :], ln_b[i, 0][None, :],
                            (vec(i, 3), vec(i, 4)), alpha=alpha, n_lat_tiles=n_lat_tiles, n_tiles=n_tiles,
                            name="mixer_out_ln")
        act = _ffn_in(h.reshape(b * t, d), ffn_w_in, i, MM_ROW_TILE, 512).reshape(b, t, -1)
        if last:
            xs, = _outproj_ln(act, ffn_w_out[i].astype(BF16), xs, vec(i, 5), ln_g[i, 1][None, :], ln_b[i, 1][None, :],
                              (), alpha=alpha, n_lat_tiles=n_lat_tiles, n_tiles=n_lat_tiles, name="ffn_out_ln_final")
        else:
            xs, h = _outproj_ln(act, ffn_w_out[i].astype(BF16), xs, vec(i, 5), ln_g[i, 1][None, :], ln_b[i, 1][None, :],
                                (vec(i + 1, 0), vec(i + 1, 1)), alpha=alpha, n_lat_tiles=n_lat_tiles,
                                n_tiles=n_tiles, name="ffn_out_ln")
    return xs
```

```python
import functools

import jax
import jax.numpy as jnp
from jax import lax
from jax.experimental import pallas as pl
from jax.experimental.pallas import tpu as pltpu

F32 = jnp.float32
BF16 = jnp.bfloat16

GRID_W = 64
N_MIXERS = 4
RET_DK = 256
RET_DV = 512
RET_CHUNK = 128
GQA_HEAD_DIM = 128
GQA_GROUP = 4
MLA_NOPE = 128
MLA_ROPE = 64
MLA_V = 128
HGRN_DIM = 128
GLA_CHUNK = 128
ROPE_THETA = 10000.0
EPS = 1e-6

LANES = 128
ROW_TILE = 256
ATTN_KV_CHUNKS = (512, 256)
LOG2E = 1.4426950408889634
V7X_VMEM_LIMIT = 56 * 1024 * 1024


def _params(*sem):
    return pltpu.CompilerParams(dimension_semantics=sem, vmem_limit_bytes=V7X_VMEM_LIMIT)


def _dot(a, b):
    return jnp.dot(a, b, preferred_element_type=F32)


def _dot_nt(a, b):
    return lax.dot_general(a, b, (((1,), (1,)), ((), ())), preferred_element_type=F32)


def _dot_tn(a, b):
    return lax.dot_general(a, b, (((0,), (0,)), ((), ())), preferred_element_type=F32)


def _silu(x):
    return x * jax.nn.sigmoid(x)


def _swap_halves(x, w):
    if 2 * w == LANES:
        return pltpu.roll(x, w, axis=1)
    up = pltpu.roll(x, LANES - w, axis=1)
    down = pltpu.roll(x, w, axis=1)
    lane = lax.broadcasted_iota(jnp.int32, x.shape, 1)
    return jnp.where((lane % (2 * w)) < w, up, down)


def _ada_kernel(c_ref, w_ref, b_ref, o_ref):
    s = _silu(c_ref[...]).astype(BF16)
    o_ref[...] = _dot(s, w_ref[...].astype(BF16)) + b_ref[...]


def _ada_all(cc, ada_w, ada_b, tn=1024):
    depth, d, n = ada_w.shape
    rows = cc.shape[0]
    return pl.pallas_call(
        _ada_kernel,
        grid=(depth, n // tn),
        in_specs=[pl.BlockSpec((rows, d), lambda l, j: (0, 0)),
                  pl.BlockSpec((None, d, tn), lambda l, j: (l, 0, j)),
                  pl.BlockSpec((None, 1, tn), lambda l, j: (l, 0, j))],
        out_specs=pl.BlockSpec((None, rows, tn), lambda l, j: (l, 0, j)),
        out_shape=jax.ShapeDtypeStruct((depth, rows, n), F32),
        compiler_params=_params("arbitrary", "arbitrary"),
        name="ada_mod",
    )(cc, ada_w, ada_b.reshape(depth, 1, n))


def _mod_kernel(x_ref, ctx_ref, sh_ref, sc_ref, xs_ref, h_ref, *, n_lat_tiles):
    def emit(src_ref):
        xv = src_ref[...]
        xs_ref[...] = xv
        h_ref[...] = (xv * (1.0 + sc_ref[...]) + sh_ref[...]).astype(BF16)

    pl.when(pl.program_id(1) < n_lat_tiles)(lambda: emit(x_ref))
    pl.when(pl.program_id(1) >= n_lat_tiles)(lambda: emit(ctx_ref))


def _modulate0(x, ctx, sh, sc):
    b, n_lat, d = x.shape
    n_lat_tiles = n_lat // ROW_TILE
    n_tiles = n_lat_tiles + ctx.shape[1] // ROW_TILE
    vec = pl.BlockSpec((None, None, 1, d), lambda i, j: (i, j // n_lat_tiles, 0, 0))
    tile = pl.BlockSpec((None, ROW_TILE, d), lambda i, j: (i, j, 0))
    return pl.pallas_call(
        functools.partial(_mod_kernel, n_lat_tiles=n_lat_tiles),
        grid=(b, n_tiles),
        in_specs=[pl.BlockSpec((None, ROW_TILE, d), lambda i, j: (i, jnp.minimum(j, n_lat_tiles - 1), 0)),
                  pl.BlockSpec((None, ROW_TILE, d), lambda i, j: (i, jnp.maximum(j - n_lat_tiles, 0), 0)),
                  vec, vec],
        out_specs=[tile, tile],
        out_shape=[jax.ShapeDtypeStruct((b, n_tiles * ROW_TILE, d), F32),
                   jax.ShapeDtypeStruct((b, n_tiles * ROW_TILE, d), BF16)],
        compiler_params=_params("arbitrary", "arbitrary"),
        name="modulate0",
    )(x, ctx, sh, sc)


def _with_bf16_weights(body, n_w):
    def kern(h_ref, *refs):
        w_refs, rest, wbf_refs = refs[:n_w], refs[n_w:len(refs) - n_w], refs[len(refs) - n_w:]

        @pl.when(pl.program_id(1) == 0)
        def _():
            for w_ref, wbf_ref in zip(w_refs, wbf_refs):
                wbf_ref[...] = w_ref[...].astype(BF16)

        body(h_ref, *wbf_refs, *rest)
    return kern


def _mm_call(body, h, w, *, col_blk0, n_blk, tn, tm, out_dtypes, name, row_ins=(), col_ins=(), const_ins=()):
    rows, k = h.shape
    assert rows % tm == 0 and w.shape[0] == k
    in_specs = [pl.BlockSpec((tm, k), lambda j, i: (i, 0)),
                pl.BlockSpec((k, tn), lambda j, i: (0, j + col_blk0))]
    args = [h, w]
    for a in row_ins:
        in_specs.append(pl.BlockSpec((tm, a.shape[1]), lambda j, i: (i, 0)))
        args.append(a)
    for a in col_ins:
        assert a.shape[1] == n_blk * tn
        in_specs.append(pl.BlockSpec((a.shape[0], tn), lambda j, i: (0, j)))
        args.append(a)
    for a in const_ins:
        in_specs.append(pl.BlockSpec(a.shape, lambda j, i: (0, 0)))
        args.append(a)
    scratch = []
    if w.dtype != BF16:
        body = _with_bf16_weights(body, 1)
        scratch = [pltpu.VMEM((k, tn), BF16)]
    out = pl.pallas_call(
        body,
        grid=(n_blk, rows // tm),
        in_specs=in_specs,
        out_specs=[pl.BlockSpec((tm, tn), lambda j, i: (i, j)) for _ in out_dtypes],
        out_shape=[jax.ShapeDtypeStruct((rows, n_blk * tn), dt) for dt in out_dtypes],
        scratch_shapes=scratch,
        compiler_params=_params("arbitrary", "arbitrary"),
        name=name,
    )(*args)
    return out


def _plain_body(h_ref, w_ref, o_ref, *, scale):
    acc = _dot(h_ref[...], w_ref[...])
    if scale != 1.0:
        acc = acc * scale
    o_ref[...] = acc.astype(o_ref.dtype)


def _silu_body(h_ref, w_ref, o_ref):
    o_ref[...] = _silu(_dot(h_ref[...], w_ref[...])).astype(o_ref.dtype)


def _rope_body(h_ref, w_ref, cos_ref, sin_ref, gain_ref, *rest, swap, norm):
    o_ref = rest[-1]
    acc = _dot(h_ref[...], w_ref[...])
    tn = acc.shape[1]
    tw = cos_ref.shape[1]
    if norm:
        acc = acc * lax.rsqrt(_dot((acc * acc).astype(BF16), rest[0][...]) + EPS)
    for c in range(tn // LANES):
        sl = slice(c * LANES, (c + 1) * LANES)
        x = acc[:, sl] * gain_ref[:, sl]
        t = c % (tw // LANES)
        tsl = slice(t * LANES, (t + 1) * LANES)
        y = x * cos_ref[:, tsl] + _swap_halves(x, swap) * sin_ref[:, tsl]
        o_ref[:, sl] = y.astype(o_ref.dtype)


def _forget_body(h_ref, w_ref, raw_ref, k_ref, lf_ref, *, layer):
    raw = raw_ref[...]
    e = jnp.exp(raw - jnp.max(raw, axis=0, keepdims=True))
    p = e / jnp.sum(e, axis=0, keepdims=True)
    lb = jnp.zeros_like(p[0:1])
    for r in range(1, layer + 1):
        lb = lb + p[r:r + 1]
    f = lb + (1.0 - lb) * jax.nn.sigmoid(_dot(h_ref[...], w_ref[...]))
    k_ref[...] = (1.0 - f).astype(k_ref.dtype)
    lf_ref[...] = jnp.log(f)


def _mla_in_body(h_ref, w_ref, cos_ref, sin_ref, qn_ref, kvn_ref, cq_ref, ckv_ref, kr_ref, *, qr, kvr):
    acc = _dot(h_ref[...], w_ref[...])
    cq = acc[:, :qr]
    cq_ref[...] = (cq * lax.rsqrt(jnp.mean(cq * cq, axis=-1, keepdims=True) + EPS) * qn_ref[...]).astype(BF16)
    ckv = acc[:, qr:qr + kvr]
    ckv_ref[...] = (ckv * lax.rsqrt(jnp.mean(ckv * ckv, axis=-1, keepdims=True) + EPS) * kvn_ref[...]).astype(BF16)
    kr = acc[:, qr + kvr:]
    kr_ref[...] = (kr * cos_ref[...] + _swap_halves(kr, MLA_ROPE // 4) * sin_ref[...]).astype(BF16)


def _swiglu_body(h_ref, wa_ref, wb_ref, o_ref):
    h = h_ref[...]
    a = _dot(h, wa_ref[...])
    b = _dot(h, wb_ref[...])
    o_ref[...] = (_silu(a) * b).astype(o_ref.dtype)


def _ffn_in(h, w_all, layer, tm, tn):
    rows, k = h.shape
    hidden = w_all.shape[2] // 2
    nb = hidden // tn
    w = w_all
    return pl.pallas_call(
        _with_bf16_weights(_swiglu_body, 2),
        grid=(nb, rows // tm),
        in_specs=[pl.BlockSpec((tm, k), lambda j, i: (i, 0)),
                  pl.BlockSpec((None, k, tn), lambda j, i: (layer, 0, j)),
                  pl.BlockSpec((None, k, tn), lambda j, i: (layer, 0, j + nb))],
        out_specs=pl.BlockSpec((tm, tn), lambda j, i: (i, j)),
        out_shape=jax.ShapeDtypeStruct((rows, hidden), BF16),
        scratch_shapes=[pltpu.VMEM((k, tn), BF16), pltpu.VMEM((k, tn), BF16)],
        compiler_params=_params("arbitrary", "arbitrary"),
        name="ffn_in_swiglu",
    )(h, w, w)


def _outproj_ln_kernel(a_ref, w_ref, x_ref, gate_ref, lng_ref, lnb_ref, *rest, alpha):
    y = _dot(a_ref[...], w_ref[...])
    z = alpha * x_ref[...] + gate_ref[...] * y
    zc = z - jnp.mean(z, axis=-1, keepdims=True)
    var = jnp.mean(zc * zc, axis=-1, keepdims=True)
    xn = zc * lax.rsqrt(var + EPS) * lng_ref[...] + lnb_ref[...]
    if len(rest) == 1:
        rest[0][...] = xn
    else:
        sh_ref, sc_ref, xo_ref, ho_ref = rest
        xo_ref[...] = xn
        ho_ref[...] = (xn * (1.0 + sc_ref[...]) + sh_ref[...]).astype(BF16)


def _outproj_ln(a, w, x, gate, lng, lnb, next_mod, *, alpha, n_lat_tiles, n_tiles, name):
    b, _, k = a.shape
    d = w.shape[1]
    tile = lambda width: pl.BlockSpec((None, ROW_TILE, width), lambda i, j: (i, j, 0))
    vec = pl.BlockSpec((None, None, 1, d), lambda i, j: (i, j // n_lat_tiles, 0, 0))
    full = pl.BlockSpec((1, d), lambda i, j: (0, 0))
    out_shape = [jax.ShapeDtypeStruct((b, n_tiles * ROW_TILE, d), F32)]
    if next_mod:
        out_shape.append(jax.ShapeDtypeStruct((b, n_tiles * ROW_TILE, d), BF16))
    return pl.pallas_call(
        functools.partial(_outproj_ln_kernel, alpha=alpha),
        grid=(b, n_tiles),
        in_specs=[tile(k),
                  pl.BlockSpec((k, d), lambda i, j: (0, 0), pipeline_mode=pl.Buffered(1)),
                  tile(d), vec, full, full] + [vec for _ in next_mod],
        out_specs=[tile(d) for _ in out_shape],
        out_shape=out_shape,
        compiler_params=_params("arbitrary", "arbitrary"),
        name=name,
    )(a, w, x, gate, lng, lnb, *next_mod)


def _kv_chunk(n_lat):
    return next(ck for ck in ATTN_KV_CHUNKS if n_lat % ck == 0)


def _fill_vt(v, vt_ref, vtc_ref, n_lat, ck):
    dv = v.shape[1]
    vt = v.astype(F32).T.astype(BF16)
    for c in range(n_lat // ck):
        vt_ref[c, 0:dv, :] = vt[:, c * ck:(c + 1) * ck]
        vt_ref[c, dv:, :] = jnp.ones((vt_ref.shape[1] - dv, ck), BF16)
    vtc_ref[0:dv, :] = vt[:, n_lat:]
    vtc_ref[dv:, :] = jnp.ones((vtc_ref.shape[0] - dv, vtc_ref.shape[1]), BF16)


def _attend(streams, is_ctx, n_chunks, m_ref, acc_ref, sa_ref, sb_ref):
    assert n_chunks % 2 == 0
    n = len(streams)
    dv = acc_ref.shape[1] // 2
    m_ref[...] = jnp.full(m_ref.shape, -jnp.inf, F32)
    acc_ref[...] = jnp.zeros(acc_ref.shape, F32)

    def accumulate(h, s, vtc):
        m_old = m_ref[h]
        m_new = jnp.maximum(m_old, jnp.max(s, axis=0, keepdims=True))
        p = jnp.exp2(s - m_new).astype(BF16)
        acc_ref[h] = jnp.exp2(m_old - m_new) * acc_ref[h] + _dot(vtc, p)
        m_ref[h] = m_new

    n_ctx = streams[0][2].shape[0]

    def stage(c_next, buf_next, c_cur, buf_cur):
        for h, (qs, k_lat, k_ctx, vt_ref, _) in enumerate(streams):
            if c_next == "ctx":
                buf_next[h, 0:n_ctx, :] = _dot_nt(k_ctx, qs)
            elif c_next is not None:
                buf_next[h] = _dot_nt(k_lat(c_next), qs)
            if c_cur is not None:
                accumulate(h, buf_cur[h], vt_ref[c_cur])

    @pl.when(jnp.logical_not(is_ctx))
    def _():
        stage(0, sa_ref, None, None)

        def pair(i, carry):
            stage(2 * i + 1, sb_ref, 2 * i, sa_ref)
            stage(2 * i + 2, sa_ref, 2 * i + 1, sb_ref)
            return carry

        lax.fori_loop(0, n_chunks // 2 - 1, pair, 0)
        stage(n_chunks - 1, sb_ref, n_chunks - 2, sa_ref)
        stage("ctx", sa_ref, n_chunks - 1, sb_ref)

    @pl.when(is_ctx)
    def _():
        stage("ctx", sa_ref, None, None)

    outs = []
    for h, (_, _, _, _, vtc_ref) in enumerate(streams):
        accumulate(h, sa_ref[h, 0:n_ctx, :], vtc_ref[...])
        acc = acc_ref[h]
        outs.append(acc[:dv] / acc[dv:])
    return outs


def _gqa_attn_kernel(q_ref, k_ref, v_ref, o_ref, vt_ref, vtc_ref, m_ref, acc_ref, sa_ref, sb_ref, *, n_lat, ck):
    d = GQA_HEAD_DIM
    tq = q_ref.shape[0]

    @pl.when(pl.program_id(2) == 0)
    def _():
        _fill_vt(v_ref[...], vt_ref, vtc_ref, n_lat, ck)

    is_ctx = pl.program_id(2) >= n_lat // tq
    q = q_ref[...]
    qs = jnp.concatenate([q[:, g * d:(g + 1) * d] for g in range(GQA_GROUP)], axis=0)
    k_lat = lambda c: k_ref[pl.ds(pl.multiple_of(c * ck, ck), ck), :]
    o_t, = _attend([(qs, k_lat, k_ref[n_lat:, :], vt_ref, vtc_ref)], is_ctx, n_lat // ck,
                   m_ref, acc_ref, sa_ref, sb_ref)
    for g in range(GQA_GROUP):
        o_ref[:, g * d:(g + 1) * d] = o_t[:, g * tq:(g + 1) * tq].T.astype(o_ref.dtype)


def _gqa_attention(qk, v, n_lat):
    b, t, _ = qk.shape
    d = GQA_HEAD_DIM
    hkv = v.shape[2] // d
    hq = hkv * GQA_GROUP
    gw = GQA_GROUP * d
    ck = _kv_chunk(n_lat)
    rows = GQA_GROUP * ROW_TILE
    return pl.pallas_call(
        functools.partial(_gqa_attn_kernel, n_lat=n_lat, ck=ck),
        grid=(b, hkv, t // ROW_TILE),
        in_specs=[pl.BlockSpec((None, ROW_TILE, gw), lambda i, h, j: (i, j, h)),
                  pl.BlockSpec((None, t, d), lambda i, h, j: (i, 0, hq + h)),
                  pl.BlockSpec((None, t, d), lambda i, h, j: (i, 0, h))],
        out_specs=pl.BlockSpec((None, ROW_TILE, gw), lambda i, h, j: (i, j, h)),
        out_shape=jax.ShapeDtypeStruct((b, t, hq * d), BF16),
        scratch_shapes=[pltpu.VMEM((n_lat // ck, 2 * d, ck), BF16), pltpu.VMEM((2 * d, t - n_lat), BF16),
                        pltpu.VMEM((1, 1, rows), F32), pltpu.VMEM((1, 2 * d, rows), F32),
                        pltpu.VMEM((1, ck, rows), F32), pltpu.VMEM((1, ck, rows), F32)],
        compiler_params=_params("arbitrary", "arbitrary", "arbitrary"),
        name="gqa_attention",
    )(qk, qk, v)


MLA_HEAD_GROUP = 4


def _mla_attn_kernel(qn_ref, qr_ref, kn_ref, kr_ref, v_ref, o_ref, vt_ref, vtc_ref, m_ref, acc_ref, sa_ref, sb_ref, *,
                     n_lat, ck):
    tq = qn_ref.shape[0]
    n_chunks = n_lat // ck

    @pl.when(pl.program_id(2) == 0)
    def _():
        for hh in range(MLA_HEAD_GROUP):
            _fill_vt(v_ref[:, hh * MLA_V:(hh + 1) * MLA_V], vt_ref.at[pl.ds(hh * n_chunks, n_chunks)],
                     vtc_ref.at[hh], n_lat, ck)

    is_ctx = pl.program_id(2) >= n_lat // tq
    lane = lax.broadcasted_iota(jnp.int32, (tq, LANES), 1)
    streams = []
    for hh in range(MLA_HEAD_GROUP):
        nope = slice(hh * MLA_NOPE, (hh + 1) * MLA_NOPE)
        pair = qr_ref[:, (hh // 2) * LANES:(hh // 2 + 1) * LANES]
        qr = jnp.where((lane // MLA_ROPE) == (hh % 2), pair, jnp.zeros_like(pair))
        qs = jnp.concatenate([qn_ref[:, nope], qr], axis=1)

        def k_lat(c, nope=nope):
            sl = pl.ds(pl.multiple_of(c * ck, ck), ck)
            return jnp.concatenate([kn_ref[sl, nope], kr_ref[sl, :]], axis=1)

        k_ctx = jnp.concatenate([kn_ref[n_lat:, nope], kr_ref[n_lat:, :]], axis=1)
        streams.append((qs, k_lat, k_ctx, vt_ref.at[pl.ds(hh * n_chunks, n_chunks)], vtc_ref.at[hh]))
    outs = _attend(streams, is_ctx, n_chunks, m_ref, acc_ref, sa_ref, sb_ref)
    for hh, o_t in enumerate(outs):
        o_ref[:, hh * MLA_V:(hh + 1) * MLA_V] = o_t.T.astype(o_ref.dtype)


def _mla_attention(q, kv, kr, n_lat, heads):
    b, t, _ = q.shape
    hg = MLA_HEAD_GROUP
    ck = _kv_chunk(n_lat)
    n_groups = heads // hg
    rope_w = hg * MLA_ROPE
    return pl.pallas_call(
        functools.partial(_mla_attn_kernel, n_lat=n_lat, ck=ck),
        grid=(b, n_groups, t // ROW_TILE),
        in_specs=[pl.BlockSpec((None, ROW_TILE, hg * MLA_NOPE), lambda i, h, j: (i, j, h)),
                  pl.BlockSpec((None, ROW_TILE, rope_w), lambda i, h, j: (i, j, heads * MLA_NOPE // rope_w + h)),
                  pl.BlockSpec((None, t, hg * MLA_NOPE), lambda i, h, j: (i, 0, h)),
                  pl.BlockSpec((None, t, LANES), lambda i, h, j: (i, 0, 0)),
                  pl.BlockSpec((None, t, hg * MLA_V), lambda i, h, j: (i, 0, n_groups + h))],
        out_specs=pl.BlockSpec((None, ROW_TILE, hg * MLA_V), lambda i, h, j: (i, j, h)),
        out_shape=jax.ShapeDtypeStruct((b, t, heads * MLA_V), BF16),
        scratch_shapes=[pltpu.VMEM((hg * (n_lat // ck), 2 * MLA_V, ck), BF16),
                        pltpu.VMEM((hg, 2 * MLA_V, t - n_lat), BF16),
                        pltpu.VMEM((hg, 1, ROW_TILE), F32), pltpu.VMEM((hg, 2 * MLA_V, ROW_TILE), F32),
                        pltpu.VMEM((hg, ck, ROW_TILE), F32), pltpu.VMEM((hg, ck, ROW_TILE), F32)],
        compiler_params=_params("arbitrary", "arbitrary", "arbitrary"),
        name="mla_attention",
    )(q, q, kv, kr, kv)


def _chunk_order(step, n_lat_chunks, n_chunks, reverse):
    if reverse:
        return n_chunks - 1 - step
    n_ctx = n_chunks - n_lat_chunks
    return jnp.where(step < n_ctx, n_lat_chunks + step, step - n_ctx)


def _ret_kernel(lg_ref, q_ref, k_ref, v_ref, *rest, reverse, heads):
    if reverse:
        of_ref, g_ref, o_ref, s_ref, mask_ref, qd_ref, kd_ref = rest
    else:
        o_ref, s_ref, mask_ref, qd_ref, kd_ref = rest
    c = RET_CHUNK

    @pl.when(pl.program_id(1) == 0)
    def _():
        s_ref[...] = jnp.zeros(s_ref.shape, F32)
        i = lax.broadcasted_iota(jnp.int32, (c, c), 0)
        j = lax.broadcasted_iota(jnp.int32, (c, c), 1)
        diff = (j - i) if reverse else (i - j)
        pos = lax.broadcasted_iota(jnp.int32, (c, RET_DK), 0)
        pos = (c - 1 - pos) if reverse else pos
        for h in range(heads):
            lg = lg_ref[h]
            mask_ref[h] = jnp.where(diff >= 0, jnp.exp(jnp.maximum(diff, 0).astype(F32) * lg), 0.0)
            qd_ref[h] = jnp.exp((pos + 1).astype(F32) * lg)
            kd_ref[h] = jnp.exp((c - 1 - pos).astype(F32) * lg)

    for h in range(heads):
        ksl = slice(h * RET_DK, (h + 1) * RET_DK)
        vsl = slice(h * RET_DV, (h + 1) * RET_DV)
        q = q_ref[:, ksl]
        k = k_ref[:, ksl]
        v = v_ref[:, vsl]
        s_old = s_ref[h]
        scores = _dot_nt(q, k) * mask_ref[h]
        o = _dot(scores.astype(BF16), v) + _dot((q.astype(F32) * qd_ref[h]).astype(BF16), s_old.astype(BF16))
        chunk_decay = jnp.exp(jnp.zeros((1, RET_DV), F32) + float(c) * lg_ref[h])
        s_ref[h] = s_old * chunk_decay + _dot_tn((k.astype(F32) * kd_ref[h]).astype(BF16), v)
        if reverse:
            o = o + of_ref[:, vsl]
            o = o * lax.rsqrt(jnp.mean(o * o, axis=-1, keepdims=True) + EPS)
            o_ref[:, vsl] = (o * _silu(g_ref[:, vsl].astype(F32))).astype(o_ref.dtype)
        else:
            o_ref[:, vsl] = o


def _retention(qk, vg, lg_fwd, lg_bwd, n_lat):
    b, t, _ = qk.shape
    heads = qk.shape[2] // (2 * RET_DK)
    c = RET_CHUNK
    n_chunks, n_lat_chunks = t // c, n_lat // c
    wk, wv = heads * RET_DK, heads * RET_DV

    def call(reverse, lg, extra_args, extra_specs, out_dtype):
        order = functools.partial(_chunk_order, n_lat_chunks=n_lat_chunks, n_chunks=n_chunks, reverse=reverse)
        blk = lambda width, col: pl.BlockSpec((None, c, width), lambda i, s: (i, order(s), col))
        return pl.pallas_call(
            functools.partial(_ret_kernel, reverse=reverse, heads=heads),
            grid=(b, n_chunks),
            in_specs=[pl.BlockSpec(memory_space=pltpu.SMEM), blk(wk, 0), blk(wk, 1), blk(wv, 0)]
                     + [blk(wv, col) for col in extra_specs],
            out_specs=blk(wv, 0),
            out_shape=jax.ShapeDtypeStruct((b, t, wv), out_dtype),
            scratch_shapes=[pltpu.VMEM((heads, RET_DK, RET_DV), F32), pltpu.VMEM((heads, c, c), F32),
                            pltpu.VMEM((heads, c, RET_DK), F32), pltpu.VMEM((heads, c, RET_DK), F32)],
            compiler_params=_params("arbitrary", "arbitrary"),
            name="retention_bwd" if reverse else "retention_fwd",
        )(lg, qk, qk, vg, *extra_args)

    o_f = call(False, lg_fwd, (), (), F32)
    return call(True, lg_bwd, (o_f, vg), (0, 1), BF16)


GLA_BATCH_GROUP = 2


def _gla_kernel(q_ref, k_ref, lf_ref, v_ref, *rest, reverse, heads):
    if reverse:
        of_ref, g_ref, gain_ref, o_ref, st_ref = rest
    else:
        o_ref, st_ref = rest
    c = GLA_CHUNK
    d = HGRN_DIM

    @pl.when(pl.program_id(1) == 0)
    def _():
        st_ref[...] = jnp.zeros(st_ref.shape, F32)

    i = lax.broadcasted_iota(jnp.int32, (c, c), 0)
    j = lax.broadcasted_iota(jnp.int32, (c, c), 1)
    tri = (j >= i) if reverse else (i >= j)
    tri_bf = jnp.where(tri, 1.0, 0.0).astype(BF16)
    mid = c // 2 if reverse else c // 2 - 1
    for n in range(q_ref.shape[0]):
        lf = lf_ref[n]
        lf_hi = lf.astype(BF16)
        lf_lo = (lf - lf_hi.astype(F32)).astype(BF16)
        bcum = _dot(tri_bf, lf_hi) + _dot(tri_bf, lf_lo)
        b_last = bcum[0:1, :] if reverse else bcum[c - 1:c, :]
        b_mid = bcum[mid:mid + 1, :]
        qf = q_ref[n].astype(F32)
        kf = k_ref[n].astype(F32)
        q_in = (qf * jnp.exp(bcum - b_mid)).astype(BF16)
        k_in = (kf * jnp.exp(b_mid - bcum)).astype(BF16)
        q_st = (qf * jnp.exp(bcum)).astype(BF16)
        k_out = (kf * jnp.exp(b_last - bcum)).astype(BF16)
        decay = jnp.exp(b_last)
        v = v_ref[n]
        for h in range(heads):
            sl = slice(h * d, (h + 1) * d)
            st = st_ref[n, h]
            scores = jnp.where(tri, _dot_nt(q_in[:, sl], k_in[:, sl]), 0.0)
            o = _dot_nt(q_st[:, sl], st.astype(BF16)) + _dot(scores.astype(BF16), v[:, sl])
            st_ref[n, h] = st * decay[:, sl] + _dot_tn(v[:, sl], k_out[:, sl])
            if reverse:
                o = o + of_ref[n, :, sl]
                o = o * lax.rsqrt(jnp.mean(o * o, axis=-1, keepdims=True) + EPS) * gain_ref[...]
                o_ref[n, :, sl] = (o * _silu(g_ref[n, :, sl].astype(F32))).astype(o_ref.dtype)
            else:
                o_ref[n, :, sl] = o


def _gla(qs, kf, lff, kb, lfb, ig, gain, n_lat, n_out_tiles=None):
    b, t, w = qs.shape
    heads = w // HGRN_DIM
    c = GLA_CHUNK
    n_chunks, n_lat_chunks = t // c, n_lat // c

    bg = GLA_BATCH_GROUP if b % GLA_BATCH_GROUP == 0 else 1

    def call(reverse, k, lf, extra_args, extra_specs, out_dtype):
        order = functools.partial(_chunk_order, n_lat_chunks=n_lat_chunks, n_chunks=n_chunks, reverse=reverse)
        blk = lambda col: pl.BlockSpec((bg, c, w), lambda i, s: (i, order(s), col))
        return pl.pallas_call(
            functools.partial(_gla_kernel, reverse=reverse, heads=heads),
            grid=(b // bg, n_chunks),
            in_specs=[blk(0), blk(0), blk(0), blk(0)] + [spec(blk) for spec in extra_specs],
            out_specs=blk(0),
            out_shape=jax.ShapeDtypeStruct((b, t, w), out_dtype),
            scratch_shapes=[pltpu.VMEM((bg, heads, HGRN_DIM, HGRN_DIM), F32)],
            compiler_params=_params("arbitrary", "arbitrary"),
            name="gla_bwd" if reverse else "gla_fwd",
        )(qs, k, lf, ig, *extra_args)

    o_f = call(False, kf, lff, (), (), F32)
    gain_spec = lambda blk: pl.BlockSpec((1, HGRN_DIM), lambda i, s: (0, 0))
    return call(True, kb, lfb, (o_f, ig, gain.reshape(1, HGRN_DIM)),
                (lambda blk: blk(0), lambda blk: blk(1), gain_spec), BF16)


def _rope_tables(rot_dim, n_lat, n_ctx, batch, width):
    half = rot_dim // 2
    freqs = ROPE_THETA ** (-jnp.arange(0, half, 2, dtype=F32) / half)
    tok = jnp.arange(n_lat)
    a_row = (tok // GRID_W).astype(F32)[:, None] * freqs
    a_col = (tok % GRID_W).astype(F32)[:, None] * freqs
    cos = jnp.concatenate([jnp.cos(a_row), jnp.cos(a_row), jnp.cos(a_col), jnp.cos(a_col)], axis=-1)
    sin = jnp.concatenate([-jnp.sin(a_row), jnp.sin(a_row), -jnp.sin(a_col), jnp.sin(a_col)], axis=-1)
    cos = jnp.concatenate([cos, jnp.ones((n_ctx, rot_dim), F32)], axis=0)
    sin = jnp.concatenate([sin, jnp.zeros((n_ctx, rot_dim), F32)], axis=0)
    reps = (batch, width // rot_dim)
    return jnp.tile(cos, reps), jnp.tile(sin, reps)


MM_ROW_TILE = 1024
MM_COL_TILES = (1024, 512, 256, 128)


def _col_tile(*widths):
    return next(tn for tn in MM_COL_TILES if all(w % tn == 0 for w in widths))


def _retention_mixer(h2, w_in, lg_fwd, lg_bwd, b, t, n_lat):
    heads = lg_fwd.shape[0]
    hk = heads * RET_DK
    w = w_in
    cos, sin = _rope_tables(RET_DK, n_lat, t - n_lat, b, RET_DK)
    gain = jnp.concatenate([jnp.full((1, hk), RET_DK ** -0.5, F32), jnp.ones((1, hk), F32)], axis=1)
    tn = _col_tile(2 * hk, w.shape[1] - 2 * hk)
    qk, = _mm_call(functools.partial(_rope_body, swap=RET_DK // 4, norm=False), h2, w,
                   col_blk0=0, n_blk=2 * hk // tn, tn=tn, tm=MM_ROW_TILE, out_dtypes=[BF16],
                   row_ins=(cos, sin), col_ins=(gain,), name="ret_in_qk")
    vg, = _mm_call(functools.partial(_plain_body, scale=1.0), h2, w,
                   col_blk0=2 * hk // tn, n_blk=(w.shape[1] - 2 * hk) // tn, tn=tn, tm=MM_ROW_TILE,
                   out_dtypes=[BF16], name="ret_in_vg")
    return _retention(qk.reshape(b, t, -1), vg.reshape(b, t, -1), lg_fwd, lg_bwd, n_lat)


def _gqa_mixer(h2, w_in, q_gain, k_gain, b, t, n_lat):
    d = GQA_HEAD_DIM
    w = w_in
    n_qk = w.shape[1] * (GQA_GROUP + 1) // (GQA_GROUP + 2)
    hq = n_qk // d * GQA_GROUP // (GQA_GROUP + 1)
    cos, sin = _rope_tables(d, n_lat, t - n_lat, b, d)
    gain = jnp.concatenate([jnp.tile(q_gain * (d ** -0.5 * LOG2E), hq), jnp.tile(k_gain, n_qk // d - hq)])[None, :]
    tn = _col_tile(n_qk, w.shape[1] - n_qk)
    head_of = jnp.arange(tn) // d
    block_mean = jnp.where(head_of[:, None] == head_of[None, :], 1.0 / d, 0.0).astype(BF16)
    qk, = _mm_call(functools.partial(_rope_body, swap=d // 4, norm=True), h2, w,
                   col_blk0=0, n_blk=n_qk // tn, tn=tn, tm=MM_ROW_TILE, out_dtypes=[BF16],
                   row_ins=(cos, sin), col_ins=(gain.astype(F32),), const_ins=(block_mean,), name="gqa_in_qk")
    v, = _mm_call(functools.partial(_plain_body, scale=1.0), h2, w,
                  col_blk0=n_qk // tn, n_blk=(w.shape[1] - n_qk) // tn, tn=tn, tm=MM_ROW_TILE,
                  out_dtypes=[BF16], name="gqa_in_v")
    return _gqa_attention(qk.reshape(b, t, -1), v.reshape(b, t, -1), n_lat)


def _mla_mixer(h2, w_in, q_norm, w_q_up, kv_norm, w_kv_up, b, t, n_lat):
    qr, kvr = q_norm.shape[0], kv_norm.shape[0]
    heads = w_q_up.shape[1] // (MLA_NOPE + MLA_ROPE)
    rows = h2.shape[0]
    w = jnp.concatenate([w_in, w_in[:, qr + kvr:]], axis=1).astype(BF16)
    cos, sin = _rope_tables(MLA_ROPE, n_lat, t - n_lat, b, LANES)
    tm = MM_ROW_TILE
    row = lambda width: pl.BlockSpec((tm, width), lambda i: (i, 0))
    full = lambda r, c: pl.BlockSpec((r, c), lambda i: (0, 0))
    cq, ckv, kr = pl.pallas_call(
        functools.partial(_mla_in_body, qr=qr, kvr=kvr),
        grid=(rows // tm,),
        in_specs=[row(h2.shape[1]), full(*w.shape), row(LANES), row(LANES), full(1, qr), full(1, kvr)],
        out_specs=[row(qr), row(kvr), row(LANES)],
        out_shape=[jax.ShapeDtypeStruct((rows, qr), BF16), jax.ShapeDtypeStruct((rows, kvr), BF16),
                   jax.ShapeDtypeStruct((rows, LANES), BF16)],
        compiler_params=_params("arbitrary"),
        name="mla_in",
    )(h2, w, cos, sin, q_norm[None, :], kv_norm[None, :])
    wq = w_q_up.reshape(qr, heads, MLA_NOPE + MLA_ROPE)
    wq = jnp.concatenate([wq[:, :, :MLA_NOPE].reshape(qr, -1), wq[:, :, MLA_NOPE:].reshape(qr, -1)], axis=1).astype(BF16)
    wkv = w_kv_up.reshape(kvr, heads, MLA_NOPE + MLA_V)
    wkv = jnp.concatenate([wkv[:, :, :MLA_NOPE].reshape(kvr, -1), wkv[:, :, MLA_NOPE:].reshape(kvr, -1)], axis=1).astype(BF16)
    scale = (MLA_NOPE + MLA_ROPE) ** -0.5 * LOG2E
    n_nope = heads * MLA_NOPE
    n_rope = heads * MLA_ROPE
    tn = _col_tile(n_nope, n_rope)
    q_nope, = _mm_call(functools.partial(_plain_body, scale=scale), cq, wq, col_blk0=0, n_blk=n_nope // tn,
                       tn=tn, tm=tm, out_dtypes=[BF16], name="mla_q_nope")
    gain = jnp.full((1, n_rope), scale, F32)
    q_rope, = _mm_call(functools.partial(_rope_body, swap=MLA_ROPE // 4, norm=False), cq, wq,
                       col_blk0=n_nope // tn, n_blk=n_rope // tn, tn=tn, tm=tm, out_dtypes=[BF16],
                       row_ins=(cos, sin), col_ins=(gain,), name="mla_q_rope")
    q = jnp.concatenate([q_nope, q_rope], axis=1)
    tn = _col_tile(wkv.shape[1])
    kv, = _mm_call(functools.partial(_plain_body, scale=1.0), ckv, wkv, col_blk0=0, n_blk=wkv.shape[1] // tn,
                   tn=tn, tm=tm, out_dtypes=[BF16], name="mla_kv_up")
    return _mla_attention(q.reshape(b, t, -1), kv.reshape(b, t, -1), kr.reshape(b, t, -1), n_lat, heads)


def _hgrn_mixer(h2, w_in, lb_raw, out_gain, layer, b, t, n_lat):
    w = w_in
    width = w.shape[1] // 5
    tn, tm = _col_tile(width), MM_ROW_TILE
    nb = width // tn
    qs, = _mm_call(_silu_body, h2, w, col_blk0=0, n_blk=nb, tn=tn, tm=tm, out_dtypes=[BF16], name="hgrn_in_q")
    forget = functools.partial(_forget_body, layer=layer)
    kf, lff = _mm_call(forget, h2, w, col_blk0=nb, n_blk=nb, tn=tn, tm=tm, out_dtypes=[BF16, F32],
                       col_ins=(lb_raw,), name="hgrn_in_ff")
    kb, lfb = _mm_call(forget, h2, w, col_blk0=2 * nb, n_blk=nb, tn=tn, tm=tm, out_dtypes=[BF16, F32],
                       col_ins=(lb_raw,), name="hgrn_in_fb")
    ig, = _mm_call(functools.partial(_plain_body, scale=1.0), h2, w, col_blk0=3 * nb, n_blk=2 * nb, tn=tn, tm=tm,
                   out_dtypes=[BF16], name="hgrn_in_ig")
    r3 = lambda a: a.reshape(b, t, -1)
    return _gla(r3(qs), r3(kf), r3(lff), r3(kb), r3(lfb), r3(ig), out_gain, n_lat)


def kernel(x, c, ctx, c_ctx, ada_w, ada_b, ln_g, ln_b, ffn_w_in, ffn_w_out, ret_w_in, ret_decay_fwd, ret_decay_bwd, ret_w_out, gqa_w_in, gqa_q_norm, gqa_k_norm, gqa_w_out, mla_w_in, mla_q_norm, mla_w_q_up, mla_kv_norm, mla_w_kv_up, mla_w_out, hgrn_w_in, hgrn_lb_raw, hgrn_out_norm, hgrn_w_out):
    b, n_lat, d = x.shape
    n_ctx = ctx.shape[1]
    t = n_lat + n_ctx
    depth = ada_w.shape[0]
    alpha = (2 * depth) ** 0.25
    n_lat_tiles = n_lat // ROW_TILE
    n_tiles = t // ROW_TILE

    pad = (-(b + 1)) % 8
    cc = jnp.concatenate([c, c_ctx[None, :], jnp.zeros((pad, d), F32)], axis=0)
    mod = _ada_all(cc, ada_w, ada_b)

    def vec(layer, idx):
        m = mod[layer, :, idx * d:(idx + 1) * d]
        return jnp.stack([m[:b], jnp.broadcast_to(m[b], (b, d))], axis=1)[:, :, None, :]

    xs, h = _modulate0(x, ctx, vec(0, 0), vec(0, 1))

    for i in range(depth):
        m, j = i % N_MIXERS, i // N_MIXERS
        last = i == depth - 1
        n_out_tiles = n_lat_tiles if last else n_tiles
        h2 = h.reshape(b * t, d)
        if m == 0:
            a = _retention_mixer(h2, ret_w_in[j], ret_decay_fwd[j], ret_decay_bwd[j], b, t, n_lat)
            w_out = ret_w_out[j]
        elif m == 1:
            a = _gqa_mixer(h2, gqa_w_in[j], gqa_q_norm[j], gqa_k_norm[j], b, t, n_lat)
            w_out = gqa_w_out[j]
        elif m == 2:
            a = _mla_mixer(h2, mla_w_in[j], mla_q_norm[j], mla_w_q_up[j], mla_kv_norm[j], mla_w_kv_up[j], b, t, n_lat)
            w_out = mla_w_out[j]
        else:
            a = _hgrn_mixer(h2, hgrn_w_in[j], hgrn_lb_raw, hgrn_out_norm[j], i, b, t, n_lat)
            w_out = hgrn_w_out[j]
        xs, h = _outproj_ln(a, w_out.astype(BF16), xs, vec(i, 2), ln_g[i, 0][None, :], ln_b[i, 0][None, :],
                            (vec(i, 3), vec(i, 4)), alpha=alpha, n_lat_tiles=n_lat_tiles, n_tiles=n_out_tiles,
                            name="mixer_out_ln")
        rows_out = n_out_tiles * ROW_TILE
        act = _ffn_in(h.reshape(b * rows_out, d), ffn_w_in, i, MM_ROW_TILE, 512).reshape(b, rows_out, -1)
        if last:
            xs, = _outproj_ln(act, ffn_w_out[i].astype(BF16), xs, vec(i, 5), ln_g[i, 1][None, :], ln_b[i, 1][None, :],
                              (), alpha=alpha, n_lat_tiles=n_lat_tiles, n_tiles=n_lat_tiles, name="ffn_out_ln_final")
        else:
            xs, h = _outproj_ln(act, ffn_w_out[i].astype(BF16), xs, vec(i, 5), ln_g[i, 1][None, :], ln_b[i, 1][None, :],
                                (vec(i + 1, 0), vec(i + 1, 1)), alpha=alpha, n_lat_tiles=n_lat_tiles,
                                n_tiles=n_tiles, name="ffn_out_ln")
    return xs
```

```python
import functools

import jax
import jax.numpy as jnp
from jax import lax
from jax.experimental import pallas as pl
from jax.experimental.pallas import tpu as pltpu

F32 = jnp.float32
BF16 = jnp.bfloat16

GRID_W = 64
N_MIXERS = 4
RET_DK = 256
RET_DV = 512
RET_CHUNK = 128
GQA_HEAD_DIM = 128
GQA_GROUP = 4
MLA_NOPE = 128
MLA_ROPE = 64
MLA_V = 128
HGRN_DIM = 128
GLA_CHUNK = 128
ROPE_THETA = 10000.0
EPS = 1e-6

LANES = 128
ROW_TILE = 256
ATTN_KV_CHUNKS = (512, 256)
LOG2E = 1.4426950408889634
V7X_VMEM_LIMIT = 56 * 1024 * 1024


def _params(*sem):
    return pltpu.CompilerParams(dimension_semantics=sem, vmem_limit_bytes=V7X_VMEM_LIMIT)


def _dot(a, b):
    return jnp.dot(a, b, preferred_element_type=F32)


def _dot_nt(a, b):
    return lax.dot_general(a, b, (((1,), (1,)), ((), ())), preferred_element_type=F32)


def _dot_tn(a, b):
    return lax.dot_general(a, b, (((0,), (0,)), ((), ())), preferred_element_type=F32)


def _silu(x):
    return x * jax.nn.sigmoid(x)


def _swap_halves(x, w):
    if 2 * w == LANES:
        return pltpu.roll(x, w, axis=1)
    up = pltpu.roll(x, LANES - w, axis=1)
    down = pltpu.roll(x, w, axis=1)
    lane = lax.broadcasted_iota(jnp.int32, x.shape, 1)
    return jnp.where((lane % (2 * w)) < w, up, down)


def _ada_kernel(c_ref, w_ref, b_ref, o_ref):
    s = _silu(c_ref[...]).astype(BF16)
    o_ref[...] = _dot(s, w_ref[...].astype(BF16)) + b_ref[...]


def _ada_all(cc, ada_w, ada_b, tn=1024):
    depth, d, n = ada_w.shape
    rows = cc.shape[0]
    return pl.pallas_call(
        _ada_kernel,
        grid=(depth, n // tn),
        in_specs=[pl.BlockSpec((rows, d), lambda l, j: (0, 0)),
                  pl.BlockSpec((None, d, tn), lambda l, j: (l, 0, j)),
                  pl.BlockSpec((None, 1, tn), lambda l, j: (l, 0, j))],
        out_specs=pl.BlockSpec((None, rows, tn), lambda l, j: (l, 0, j)),
        out_shape=jax.ShapeDtypeStruct((depth, rows, n), F32),
        compiler_params=_params("arbitrary", "arbitrary"),
        name="ada_mod",
    )(cc, ada_w, ada_b.reshape(depth, 1, n))


def _mod_kernel(x_ref, ctx_ref, sh_ref, sc_ref, xs_ref, h_ref, *, n_lat_tiles):
    def emit(src_ref):
        xv = src_ref[...]
        xs_ref[...] = xv
        h_ref[...] = (xv * (1.0 + sc_ref[...]) + sh_ref[...]).astype(BF16)

    pl.when(pl.program_id(1) < n_lat_tiles)(lambda: emit(x_ref))
    pl.when(pl.program_id(1) >= n_lat_tiles)(lambda: emit(ctx_ref))


def _modulate0(x, ctx, sh, sc):
    b, n_lat, d = x.shape
    n_lat_tiles = n_lat // ROW_TILE
    n_tiles = n_lat_tiles + ctx.shape[1] // ROW_TILE
    vec = pl.BlockSpec((None, None, 1, d), lambda i, j: (i, j // n_lat_tiles, 0, 0))
    tile = pl.BlockSpec((None, ROW_TILE, d), lambda i, j: (i, j, 0))
    return pl.pallas_call(
        functools.partial(_mod_kernel, n_lat_tiles=n_lat_tiles),
        grid=(b, n_tiles),
        in_specs=[pl.BlockSpec((None, ROW_TILE, d), lambda i, j: (i, jnp.minimum(j, n_lat_tiles - 1), 0)),
                  pl.BlockSpec((None, ROW_TILE, d), lambda i, j: (i, jnp.maximum(j - n_lat_tiles, 0), 0)),
                  vec, vec],
        out_specs=[tile, tile],
        out_shape=[jax.ShapeDtypeStruct((b, n_tiles * ROW_TILE, d), F32),
                   jax.ShapeDtypeStruct((b, n_tiles * ROW_TILE, d), BF16)],
        compiler_params=_params("arbitrary", "arbitrary"),
        name="modulate0",
    )(x, ctx, sh, sc)


def _with_bf16_weights(body, n_w):
    def kern(h_ref, *refs):
        w_refs, rest, wbf_refs = refs[:n_w], refs[n_w:len(refs) - n_w], refs[len(refs) - n_w:]

        @pl.when(pl.program_id(1) == 0)
        def _():
            for w_ref, wbf_ref in zip(w_refs, wbf_refs):
                wbf_ref[...] = w_ref[...].astype(BF16)

        body(h_ref, *wbf_refs, *rest)
    return kern


def _mm_call(body, h, w, *, col_blk0, n_blk, tn, tm, out_dtypes, name, row_ins=(), col_ins=(), const_ins=()):
    rows, k = h.shape
    assert rows % tm == 0 and w.shape[0] == k
    in_specs = [pl.BlockSpec((tm, k), lambda j, i: (i, 0)),
                pl.BlockSpec((k, tn), lambda j, i: (0, j + col_blk0))]
    args = [h, w]
    for a in row_ins:
        in_specs.append(pl.BlockSpec((tm, a.shape[1]), lambda j, i: (i, 0)))
        args.append(a)
    for a in col_ins:
        assert a.shape[1] == n_blk * tn
        in_specs.append(pl.BlockSpec((a.shape[0], tn), lambda j, i: (0, j)))
        args.append(a)
    for a in const_ins:
        in_specs.append(pl.BlockSpec(a.shape, lambda j, i: (0, 0)))
        args.append(a)
    scratch = []
    if w.dtype != BF16:
        body = _with_bf16_weights(body, 1)
        scratch = [pltpu.VMEM((k, tn), BF16)]
    out = pl.pallas_call(
        body,
        grid=(n_blk, rows // tm),
        in_specs=in_specs,
        out_specs=[pl.BlockSpec((tm, tn), lambda j, i: (i, j)) for _ in out_dtypes],
        out_shape=[jax.ShapeDtypeStruct((rows, n_blk * tn), dt) for dt in out_dtypes],
        scratch_shapes=scratch,
        compiler_params=_params("arbitrary", "arbitrary"),
        name=name,
    )(*args)
    return out


def _plain_body(h_ref, w_ref, o_ref, *, scale):
    acc = _dot(h_ref[...], w_ref[...])
    if scale != 1.0:
        acc = acc * scale
    o_ref[...] = acc.astype(o_ref.dtype)


def _silu_body(h_ref, w_ref, o_ref):
    o_ref[...] = _silu(_dot(h_ref[...], w_ref[...])).astype(o_ref.dtype)


def _rope_body(h_ref, w_ref, cos_ref, sin_ref, gain_ref, *rest, swap, norm):
    o_ref = rest[-1]
    acc = _dot(h_ref[...], w_ref[...])
    tn = acc.shape[1]
    tw = cos_ref.shape[1]
    if norm:
        acc = acc * lax.rsqrt(_dot((acc * acc).astype(BF16), rest[0][...]) + EPS)
    for c in range(tn // LANES):
        sl = slice(c * LANES, (c + 1) * LANES)
        x = acc[:, sl] * gain_ref[:, sl]
        t = c % (tw // LANES)
        tsl = slice(t * LANES, (t + 1) * LANES)
        y = x * cos_ref[:, tsl] + _swap_halves(x, swap) * sin_ref[:, tsl]
        o_ref[:, sl] = y.astype(o_ref.dtype)


def _forget_body(h_ref, w_ref, raw_ref, k_ref, lf_ref, *, layer):
    raw = raw_ref[...]
    e = jnp.exp(raw - jnp.max(raw, axis=0, keepdims=True))
    p = e / jnp.sum(e, axis=0, keepdims=True)
    lb = jnp.zeros_like(p[0:1])
    for r in range(1, layer + 1):
        lb = lb + p[r:r + 1]
    f = lb + (1.0 - lb) * jax.nn.sigmoid(_dot(h_ref[...], w_ref[...]))
    k_ref[...] = (1.0 - f).astype(k_ref.dtype)
    lf_ref[...] = jnp.log(f)


def _mla_in_body(h_ref, w_ref, cos_ref, sin_ref, qn_ref, kvn_ref, cq_ref, ckv_ref, kr_ref, *, qr, kvr):
    acc = _dot(h_ref[...], w_ref[...])
    cq = acc[:, :qr]
    cq_ref[...] = (cq * lax.rsqrt(jnp.mean(cq * cq, axis=-1, keepdims=True) + EPS) * qn_ref[...]).astype(BF16)
    ckv = acc[:, qr:qr + kvr]
    ckv_ref[...] = (ckv * lax.rsqrt(jnp.mean(ckv * ckv, axis=-1, keepdims=True) + EPS) * kvn_ref[...]).astype(BF16)
    kr = acc[:, qr + kvr:]
    kr_ref[...] = (kr * cos_ref[...] + _swap_halves(kr, MLA_ROPE // 4) * sin_ref[...]).astype(BF16)


def _swiglu_body(h_ref, wa_ref, wb_ref, o_ref):
    h = h_ref[...]
    a = _dot(h, wa_ref[...])
    b = _dot(h, wb_ref[...])
    o_ref[...] = (_silu(a) * b).astype(o_ref.dtype)


def _ffn_in(h, w_all, layer, tm, tn):
    rows, k = h.shape
    hidden = w_all.shape[2] // 2
    nb = hidden // tn
    w = w_all
    return pl.pallas_call(
        _with_bf16_weights(_swiglu_body, 2),
        grid=(nb, rows // tm),
        in_specs=[pl.BlockSpec((tm, k), lambda j, i: (i, 0)),
                  pl.BlockSpec((None, k, tn), lambda j, i: (layer, 0, j)),
                  pl.BlockSpec((None, k, tn), lambda j, i: (layer, 0, j + nb))],
        out_specs=pl.BlockSpec((tm, tn), lambda j, i: (i, j)),
        out_shape=jax.ShapeDtypeStruct((rows, hidden), BF16),
        scratch_shapes=[pltpu.VMEM((k, tn), BF16), pltpu.VMEM((k, tn), BF16)],
        compiler_params=_params("arbitrary", "arbitrary"),
        name="ffn_in_swiglu",
    )(h, w, w)


def _outproj_ln_kernel(a_ref, w_ref, x_ref, gate_ref, lng_ref, lnb_ref, *rest, alpha):
    y = _dot(a_ref[...], w_ref[...])
    z = alpha * x_ref[...] + gate_ref[...] * y
    mean = jnp.mean(z, axis=-1, keepdims=True)
    var = jnp.maximum(jnp.mean(z * z, axis=-1, keepdims=True) - mean * mean, 0.0)
    t = (z - mean) * lax.rsqrt(var + EPS)
    g, b = lng_ref[...], lnb_ref[...]
    if len(rest) == 1:
        rest[0][...] = t * g + b
    else:
        sh_ref, sc_ref, xo_ref, ho_ref = rest
        xo_ref[...] = t * g + b
        sc1 = 1.0 + sc_ref[...]
        ho_ref[...] = (t * (g * sc1) + (b * sc1 + sh_ref[...])).astype(BF16)


def _outproj_ln(a, w, x, gate, lng, lnb, next_mod, *, alpha, n_lat_tiles, n_tiles, name):
    b, _, k = a.shape
    d = w.shape[1]
    tile = lambda width: pl.BlockSpec((None, ROW_TILE, width), lambda i, j: (i, j, 0))
    vec = pl.BlockSpec((None, None, 1, d), lambda i, j: (i, j // n_lat_tiles, 0, 0))
    full = pl.BlockSpec((1, d), lambda i, j: (0, 0))
    out_shape = [jax.ShapeDtypeStruct((b, n_tiles * ROW_TILE, d), F32)]
    if next_mod:
        out_shape.append(jax.ShapeDtypeStruct((b, n_tiles * ROW_TILE, d), BF16))
    return pl.pallas_call(
        functools.partial(_outproj_ln_kernel, alpha=alpha),
        grid=(b, n_tiles),
        in_specs=[tile(k),
                  pl.BlockSpec((k, d), lambda i, j: (0, 0), pipeline_mode=pl.Buffered(1)),
                  tile(d), vec, full, full] + [vec for _ in next_mod],
        out_specs=[tile(d) for _ in out_shape],
        out_shape=out_shape,
        compiler_params=_params("arbitrary", "arbitrary"),
        name=name,
    )(a, w, x, gate, lng, lnb, *next_mod)


def _kv_chunk(n_lat):
    return next(ck for ck in ATTN_KV_CHUNKS if n_lat % ck == 0)


def _fill_vt(v, vt_ref, vtc_ref, n_lat, ck):
    dv = v.shape[1]
    vt = v.astype(F32).T.astype(BF16)
    for c in range(n_lat // ck):
        vt_ref[c, 0:dv, :] = vt[:, c * ck:(c + 1) * ck]
        vt_ref[c, dv:, :] = jnp.ones((vt_ref.shape[1] - dv, ck), BF16)
    vtc_ref[0:dv, :] = vt[:, n_lat:]
    vtc_ref[dv:, :] = jnp.ones((vtc_ref.shape[0] - dv, vtc_ref.shape[1]), BF16)


def _attend(streams, is_ctx, n_chunks, m_ref, acc_ref, sa_ref, sb_ref):
    assert n_chunks % 2 == 0
    n = len(streams)
    dv = acc_ref.shape[1] // 2
    m_ref[...] = jnp.full(m_ref.shape, -jnp.inf, F32)
    acc_ref[...] = jnp.zeros(acc_ref.shape, F32)

    def accumulate(h, s, vtc):
        m_old = m_ref[h]
        m_new = jnp.maximum(m_old, jnp.max(s, axis=0, keepdims=True))
        p = jnp.exp2(s - m_new).astype(BF16)
        acc_ref[h] = jnp.exp2(m_old - m_new) * acc_ref[h] + _dot(vtc, p)
        m_ref[h] = m_new

    n_ctx = streams[0][2].shape[0]

    def stage(c_next, buf_next, c_cur, buf_cur):
        for h, (qs, k_lat, k_ctx, vt_ref, _) in enumerate(streams):
            if c_next == "ctx":
                buf_next[h, 0:n_ctx, :] = _dot_nt(k_ctx, qs)
            elif c_next is not None:
                buf_next[h] = _dot_nt(k_lat(c_next), qs)
            if c_cur is not None:
                accumulate(h, buf_cur[h], vt_ref[c_cur])

    @pl.when(jnp.logical_not(is_ctx))
    def _():
        stage(0, sa_ref, None, None)

        def pair(i, carry):
            stage(2 * i + 1, sb_ref, 2 * i, sa_ref)
            stage(2 * i + 2, sa_ref, 2 * i + 1, sb_ref)
            return carry

        lax.fori_loop(0, n_chunks // 2 - 1, pair, 0)
        stage(n_chunks - 1, sb_ref, n_chunks - 2, sa_ref)
        stage("ctx", sa_ref, n_chunks - 1, sb_ref)

    @pl.when(is_ctx)
    def _():
        stage("ctx", sa_ref, None, None)

    outs = []
    for h, (_, _, _, _, vtc_ref) in enumerate(streams):
        accumulate(h, sa_ref[h, 0:n_ctx, :], vtc_ref[...])
        acc = acc_ref[h]
        outs.append(acc[:dv] / acc[dv:])
    return outs


GQA_KV_HEADS_PER_STEP = 2


def _gqa_attn_kernel(q_ref, k_ref, v_ref, o_ref, vt_ref, vtc_ref, m_ref, acc_ref, sa_ref, sb_ref, *, n_lat, ck):
    d = GQA_HEAD_DIM
    tq = q_ref.shape[0]
    n_kv = k_ref.shape[1] // d
    n_chunks = n_lat // ck

    @pl.when(pl.program_id(2) == 0)
    def _():
        for s in range(n_kv):
            _fill_vt(v_ref[:, s * d:(s + 1) * d], vt_ref.at[pl.ds(s * n_chunks, n_chunks)], vtc_ref.at[s], n_lat, ck)

    is_ctx = pl.program_id(2) >= n_lat // tq
    streams = []
    for s in range(n_kv):
        ksl = slice(s * d, (s + 1) * d)
        qs = jnp.concatenate([q_ref[:, (s * GQA_GROUP + g) * d:(s * GQA_GROUP + g + 1) * d]
                              for g in range(GQA_GROUP)], axis=0)
        k_lat = lambda c, ksl=ksl: k_ref[pl.ds(pl.multiple_of(c * ck, ck), ck), ksl]
        streams.append((qs, k_lat, k_ref[n_lat:, ksl], vt_ref.at[pl.ds(s * n_chunks, n_chunks)], vtc_ref.at[s]))
    outs = _attend(streams, is_ctx, n_chunks, m_ref, acc_ref, sa_ref, sb_ref)
    for s, o_t in enumerate(outs):
        for g in range(GQA_GROUP):
            col = (s * GQA_GROUP + g) * d
            o_ref[:, col:col + d] = o_t[:, g * tq:(g + 1) * tq].T.astype(o_ref.dtype)


def _gqa_attention(qk, v, n_lat):
    b, t, _ = qk.shape
    d = GQA_HEAD_DIM
    hkv = v.shape[2] // d
    hq = hkv * GQA_GROUP
    n_kv = GQA_KV_HEADS_PER_STEP if hkv % GQA_KV_HEADS_PER_STEP == 0 else 1
    gw = n_kv * GQA_GROUP * d
    ck = _kv_chunk(n_lat)
    rows = GQA_GROUP * ROW_TILE
    return pl.pallas_call(
        functools.partial(_gqa_attn_kernel, n_lat=n_lat, ck=ck),
        grid=(b, hkv // n_kv, t // ROW_TILE),
        in_specs=[pl.BlockSpec((None, ROW_TILE, gw), lambda i, h, j: (i, j, h)),
                  pl.BlockSpec((None, t, n_kv * d), lambda i, h, j: (i, 0, hq // n_kv + h)),
                  pl.BlockSpec((None, t, n_kv * d), lambda i, h, j: (i, 0, h))],
        out_specs=pl.BlockSpec((None, ROW_TILE, gw), lambda i, h, j: (i, j, h)),
        out_shape=jax.ShapeDtypeStruct((b, t, hq * d), BF16),
        scratch_shapes=[pltpu.VMEM((n_kv * (n_lat // ck), 2 * d, ck), BF16),
                        pltpu.VMEM((n_kv, 2 * d, t - n_lat), BF16),
                        pltpu.VMEM((n_kv, 1, rows), F32), pltpu.VMEM((n_kv, 2 * d, rows), F32),
                        pltpu.VMEM((n_kv, ck, rows), F32), pltpu.VMEM((n_kv, ck, rows), F32)],
        compiler_params=_params("arbitrary", "arbitrary", "arbitrary"),
        name="gqa_attention",
    )(qk, qk, v)


MLA_HEAD_GROUP = 4


def _mla_attn_kernel(qn_ref, qr_ref, kn_ref, kr_ref, v_ref, o_ref, vt_ref, vtc_ref, m_ref, acc_ref, sa_ref, sb_ref, *,
                     n_lat, ck):
    tq = qn_ref.shape[0]
    n_chunks = n_lat // ck

    @pl.when(pl.program_id(2) == 0)
    def _():
        for hh in range(MLA_HEAD_GROUP):
            _fill_vt(v_ref[:, hh * MLA_V:(hh + 1) * MLA_V], vt_ref.at[pl.ds(hh * n_chunks, n_chunks)],
                     vtc_ref.at[hh], n_lat, ck)

    is_ctx = pl.program_id(2) >= n_lat // tq
    lane = lax.broadcasted_iota(jnp.int32, (tq, LANES), 1)
    streams = []
    for hh in range(MLA_HEAD_GROUP):
        nope = slice(hh * MLA_NOPE, (hh + 1) * MLA_NOPE)
        pair = qr_ref[:, (hh // 2) * LANES:(hh // 2 + 1) * LANES]
        qr = jnp.where((lane // MLA_ROPE) == (hh % 2), pair, jnp.zeros_like(pair))
        qs = jnp.concatenate([qn_ref[:, nope], qr], axis=1)

        def k_lat(c, nope=nope):
            sl = pl.ds(pl.multiple_of(c * ck, ck), ck)
            return jnp.concatenate([kn_ref[sl, nope], kr_ref[sl, :]], axis=1)

        k_ctx = jnp.concatenate([kn_ref[n_lat:, nope], kr_ref[n_lat:, :]], axis=1)
        streams.append((qs, k_lat, k_ctx, vt_ref.at[pl.ds(hh * n_chunks, n_chunks)], vtc_ref.at[hh]))
    outs = _attend(streams, is_ctx, n_chunks, m_ref, acc_ref, sa_ref, sb_ref)
    for hh, o_t in enumerate(outs):
        o_ref[:, hh * MLA_V:(hh + 1) * MLA_V] = o_t.T.astype(o_ref.dtype)


def _mla_attention(q_nope, q_rope, kv, kr, n_lat, heads):
    b, t, _ = q_nope.shape
    hg = MLA_HEAD_GROUP
    ck = _kv_chunk(n_lat)
    n_groups = heads // hg
    rope_w = hg * MLA_ROPE
    return pl.pallas_call(
        functools.partial(_mla_attn_kernel, n_lat=n_lat, ck=ck),
        grid=(b, n_groups, t // ROW_TILE),
        in_specs=[pl.BlockSpec((None, ROW_TILE, hg * MLA_NOPE), lambda i, h, j: (i, j, h)),
                  pl.BlockSpec((None, ROW_TILE, rope_w), lambda i, h, j: (i, j, h)),
                  pl.BlockSpec((None, t, hg * MLA_NOPE), lambda i, h, j: (i, 0, h)),
                  pl.BlockSpec((None, t, LANES), lambda i, h, j: (i, 0, 0)),
                  pl.BlockSpec((None, t, hg * MLA_V), lambda i, h, j: (i, 0, n_groups + h))],
        out_specs=pl.BlockSpec((None, ROW_TILE, hg * MLA_V), lambda i, h, j: (i, j, h)),
        out_shape=jax.ShapeDtypeStruct((b, t, heads * MLA_V), BF16),
        scratch_shapes=[pltpu.VMEM((hg * (n_lat // ck), 2 * MLA_V, ck), BF16),
                        pltpu.VMEM((hg, 2 * MLA_V, t - n_lat), BF16),
                        pltpu.VMEM((hg, 1, ROW_TILE), F32), pltpu.VMEM((hg, 2 * MLA_V, ROW_TILE), F32),
                        pltpu.VMEM((hg, ck, ROW_TILE), F32), pltpu.VMEM((hg, ck, ROW_TILE), F32)],
        compiler_params=_params("arbitrary", "arbitrary", "arbitrary"),
        name="mla_attention",
    )(q_nope, q_rope, kv, kr, kv)


def _chunk_order(step, n_lat_chunks, n_chunks, reverse):
    if reverse:
        return n_chunks - 1 - step
    n_ctx = n_chunks - n_lat_chunks
    return jnp.where(step < n_ctx, n_lat_chunks + step, step - n_ctx)


def _ret_kernel(lg_ref, q_ref, k_ref, v_ref, *rest, reverse, heads):
    if reverse:
        of_ref, g_ref, o_ref, s_ref, mask_ref, qd_ref, kd_ref = rest
    else:
        o_ref, s_ref, mask_ref, qd_ref, kd_ref = rest
    c = RET_CHUNK

    @pl.when(pl.program_id(1) == 0)
    def _():
        s_ref[...] = jnp.zeros(s_ref.shape, F32)
        i = lax.broadcasted_iota(jnp.int32, (c, c), 0)
        j = lax.broadcasted_iota(jnp.int32, (c, c), 1)
        diff = (j - i) if reverse else (i - j)
        pos = lax.broadcasted_iota(jnp.int32, (c, RET_DK), 0)
        pos = (c - 1 - pos) if reverse else pos
        for h in range(heads):
            lg = lg_ref[h]
            mask_ref[h] = jnp.where(diff >= 0, jnp.exp(jnp.maximum(diff, 0).astype(F32) * lg), 0.0)
            qd_ref[h] = jnp.exp((pos + 1).astype(F32) * lg)
            kd_ref[h] = jnp.exp((c - 1 - pos).astype(F32) * lg)

    for h in range(heads):
        ksl = slice(h * RET_DK, (h + 1) * RET_DK)
        vsl = slice(h * RET_DV, (h + 1) * RET_DV)
        q = q_ref[:, ksl]
        k = k_ref[:, ksl]
        v = v_ref[:, vsl]
        s_old = s_ref[h]
        scores = _dot_nt(q, k) * mask_ref[h]
        o = _dot(scores.astype(BF16), v) + _dot((q.astype(F32) * qd_ref[h]).astype(BF16), s_old.astype(BF16))
        chunk_decay = jnp.exp(jnp.zeros((1, RET_DV), F32) + float(c) * lg_ref[h])
        s_ref[h] = s_old * chunk_decay + _dot_tn((k.astype(F32) * kd_ref[h]).astype(BF16), v)
        if reverse:
            o = o + of_ref[:, vsl]
            o = o * lax.rsqrt(jnp.mean(o * o, axis=-1, keepdims=True) + EPS)
            o_ref[:, vsl] = (o * _silu(g_ref[:, vsl].astype(F32))).astype(o_ref.dtype)
        else:
            o_ref[:, vsl] = o


def _retention(qk, vg, lg_fwd, lg_bwd, n_lat):
    b, t, _ = qk.shape
    heads = qk.shape[2] // (2 * RET_DK)
    c = RET_CHUNK
    n_chunks, n_lat_chunks = t // c, n_lat // c
    wk, wv = heads * RET_DK, heads * RET_DV

    def call(reverse, lg, extra_args, extra_specs, out_dtype):
        order = functools.partial(_chunk_order, n_lat_chunks=n_lat_chunks, n_chunks=n_chunks, reverse=reverse)
        blk = lambda width, col: pl.BlockSpec((None, c, width), lambda i, s: (i, order(s), col))
        return pl.pallas_call(
            functools.partial(_ret_kernel, reverse=reverse, heads=heads),
            grid=(b, n_chunks),
            in_specs=[pl.BlockSpec(memory_space=pltpu.SMEM), blk(wk, 0), blk(wk, 1), blk(wv, 0)]
                     + [blk(wv, col) for col in extra_specs],
            out_specs=blk(wv, 0),
            out_shape=jax.ShapeDtypeStruct((b, t, wv), out_dtype),
            scratch_shapes=[pltpu.VMEM((heads, RET_DK, RET_DV), F32), pltpu.VMEM((heads, c, c), F32),
                            pltpu.VMEM((heads, c, RET_DK), F32), pltpu.VMEM((heads, c, RET_DK), F32)],
            compiler_params=_params("arbitrary", "arbitrary"),
            name="retention_bwd" if reverse else "retention_fwd",
        )(lg, qk, qk, vg, *extra_args)

    o_f = call(False, lg_fwd, (), (), F32)
    return call(True, lg_bwd, (o_f, vg), (0, 1), BF16)


GLA_BATCH_GROUP = 2


def _gla_kernel(q_ref, k_ref, lf_ref, v_ref, *rest, reverse, heads):
    if reverse:
        of_ref, g_ref, gain_ref, o_ref, st_ref = rest
    else:
        o_ref, st_ref = rest
    c = GLA_CHUNK
    d = HGRN_DIM

    @pl.when(pl.program_id(1) == 0)
    def _():
        st_ref[...] = jnp.zeros(st_ref.shape, F32)

    i = lax.broadcasted_iota(jnp.int32, (c, c), 0)
    j = lax.broadcasted_iota(jnp.int32, (c, c), 1)
    tri = (j >= i) if reverse else (i >= j)
    tri_bf = jnp.where(tri, 1.0, 0.0).astype(BF16)
    mid = c // 2 if reverse else c // 2 - 1
    for n in range(q_ref.shape[0]):
        lf = lf_ref[n]
        lf_hi = lf.astype(BF16)
        lf_lo = (lf - lf_hi.astype(F32)).astype(BF16)
        bcum = _dot(tri_bf, lf_hi) + _dot(tri_bf, lf_lo)
        b_last = bcum[0:1, :] if reverse else bcum[c - 1:c, :]
        b_mid = bcum[mid:mid + 1, :]
        qf = q_ref[n].astype(F32)
        kf = k_ref[n].astype(F32)
        q_in = (qf * jnp.exp(bcum - b_mid)).astype(BF16)
        k_in = (kf * jnp.exp(b_mid - bcum)).astype(BF16)
        q_st = (qf * jnp.exp(bcum)).astype(BF16)
        k_out = (kf * jnp.exp(b_last - bcum)).astype(BF16)
        decay = jnp.exp(b_last)
        v = v_ref[n]
        for h in range(heads):
            sl = slice(h * d, (h + 1) * d)
            st = st_ref[n, h]
            scores = jnp.where(tri, _dot_nt(q_in[:, sl], k_in[:, sl]), 0.0)
            o = _dot_nt(q_st[:, sl], st.astype(BF16)) + _dot(scores.astype(BF16), v[:, sl])
            st_ref[n, h] = st * decay[:, sl] + _dot_tn(v[:, sl], k_out[:, sl])
            if reverse:
                o = o + of_ref[n, :, sl]
                o = o * lax.rsqrt(jnp.mean(o * o, axis=-1, keepdims=True) + EPS) * gain_ref[...]
                o_ref[n, :, sl] = (o * _silu(g_ref[n, :, sl].astype(F32))).astype(o_ref.dtype)
            else:
                o_ref[n, :, sl] = o


def _gla(qs, kf, lff, kb, lfb, ig, gain, n_lat, n_out_tiles=None):
    b, t, w = qs.shape
    heads = w // HGRN_DIM
    c = GLA_CHUNK
    n_chunks, n_lat_chunks = t // c, n_lat // c

    bg = GLA_BATCH_GROUP if b % GLA_BATCH_GROUP == 0 else 1

    def call(reverse, k, lf, extra_args, extra_specs, out_dtype):
        order = functools.partial(_chunk_order, n_lat_chunks=n_lat_chunks, n_chunks=n_chunks, reverse=reverse)
        blk = lambda col: pl.BlockSpec((bg, c, w), lambda i, s: (i, order(s), col))
        return pl.pallas_call(
            functools.partial(_gla_kernel, reverse=reverse, heads=heads),
            grid=(b // bg, n_chunks),
            in_specs=[blk(0), blk(0), blk(0), blk(0)] + [spec(blk) for spec in extra_specs],
            out_specs=blk(0),
            out_shape=jax.ShapeDtypeStruct((b, t, w), out_dtype),
            scratch_shapes=[pltpu.VMEM((bg, heads, HGRN_DIM, HGRN_DIM), F32)],
            compiler_params=_params("arbitrary", "arbitrary"),
            name="gla_bwd" if reverse else "gla_fwd",
        )(qs, k, lf, ig, *extra_args)

    o_f = call(False, kf, lff, (), (), F32)
    gain_spec = lambda blk: pl.BlockSpec((1, HGRN_DIM), lambda i, s: (0, 0))
    return call(True, kb, lfb, (o_f, ig, gain.reshape(1, HGRN_DIM)),
                (lambda blk: blk(0), lambda blk: blk(1), gain_spec), BF16)


def _rope_tables(rot_dim, n_lat, n_ctx, batch, width):
    half = rot_dim // 2
    freqs = ROPE_THETA ** (-jnp.arange(0, half, 2, dtype=F32) / half)
    tok = jnp.arange(n_lat)
    a_row = (tok // GRID_W).astype(F32)[:, None] * freqs
    a_col = (tok % GRID_W).astype(F32)[:, None] * freqs
    cos = jnp.concatenate([jnp.cos(a_row), jnp.cos(a_row), jnp.cos(a_col), jnp.cos(a_col)], axis=-1)
    sin = jnp.concatenate([-jnp.sin(a_row), jnp.sin(a_row), -jnp.sin(a_col), jnp.sin(a_col)], axis=-1)
    cos = jnp.concatenate([cos, jnp.ones((n_ctx, rot_dim), F32)], axis=0)
    sin = jnp.concatenate([sin, jnp.zeros((n_ctx, rot_dim), F32)], axis=0)
    reps = (batch, width // rot_dim)
    return jnp.tile(cos, reps), jnp.tile(sin, reps)


MM_ROW_TILE = 1024
MM_COL_TILES = (1024, 512, 256, 128)


def _col_tile(*widths):
    return next(tn for tn in MM_COL_TILES if all(w % tn == 0 for w in widths))


def _retention_mixer(h2, w_in, lg_fwd, lg_bwd, b, t, n_lat):
    heads = lg_fwd.shape[0]
    hk = heads * RET_DK
    w = w_in
    cos, sin = _rope_tables(RET_DK, n_lat, t - n_lat, b, RET_DK)
    gain = jnp.concatenate([jnp.full((1, hk), RET_DK ** -0.5, F32), jnp.ones((1, hk), F32)], axis=1)
    tn = _col_tile(2 * hk, w.shape[1] - 2 * hk)
    qk, = _mm_call(functools.partial(_rope_body, swap=RET_DK // 4, norm=False), h2, w,
                   col_blk0=0, n_blk=2 * hk // tn, tn=tn, tm=MM_ROW_TILE, out_dtypes=[BF16],
                   row_ins=(cos, sin), col_ins=(gain,), name="ret_in_qk")
    vg, = _mm_call(functools.partial(_plain_body, scale=1.0), h2, w,
                   col_blk0=2 * hk // tn, n_blk=(w.shape[1] - 2 * hk) // tn, tn=tn, tm=MM_ROW_TILE,
                   out_dtypes=[BF16], name="ret_in_vg")
    return _retention(qk.reshape(b, t, -1), vg.reshape(b, t, -1), lg_fwd, lg_bwd, n_lat)


def _gqa_mixer(h2, w_in, q_gain, k_gain, b, t, n_lat):
    d = GQA_HEAD_DIM
    w = w_in
    n_qk = w.shape[1] * (GQA_GROUP + 1) // (GQA_GROUP + 2)
    hq = n_qk // d * GQA_GROUP // (GQA_GROUP + 1)
    cos, sin = _rope_tables(d, n_lat, t - n_lat, b, d)
    gain = jnp.concatenate([jnp.tile(q_gain * (d ** -0.5 * LOG2E), hq), jnp.tile(k_gain, n_qk // d - hq)])[None, :]
    tn = _col_tile(n_qk, w.shape[1] - n_qk)
    head_of = jnp.arange(tn) // d
    block_mean = jnp.where(head_of[:, None] == head_of[None, :], 1.0 / d, 0.0).astype(BF16)
    qk, = _mm_call(functools.partial(_rope_body, swap=d // 4, norm=True), h2, w,
                   col_blk0=0, n_blk=n_qk // tn, tn=tn, tm=MM_ROW_TILE, out_dtypes=[BF16],
                   row_ins=(cos, sin), col_ins=(gain.astype(F32),), const_ins=(block_mean,), name="gqa_in_qk")
    v, = _mm_call(functools.partial(_plain_body, scale=1.0), h2, w,
                  col_blk0=n_qk // tn, n_blk=(w.shape[1] - n_qk) // tn, tn=tn, tm=MM_ROW_TILE,
                  out_dtypes=[BF16], name="gqa_in_v")
    return _gqa_attention(qk.reshape(b, t, -1), v.reshape(b, t, -1), n_lat)


def _mla_mixer(h2, w_in, q_norm, w_q_up, kv_norm, w_kv_up, b, t, n_lat):
    qr, kvr = q_norm.shape[0], kv_norm.shape[0]
    heads = w_q_up.shape[1] // (MLA_NOPE + MLA_ROPE)
    rows = h2.shape[0]
    w = jnp.concatenate([w_in, w_in[:, qr + kvr:]], axis=1).astype(BF16)
    cos, sin = _rope_tables(MLA_ROPE, n_lat, t - n_lat, b, LANES)
    tm = MM_ROW_TILE
    row = lambda width: pl.BlockSpec((tm, width), lambda i: (i, 0))
    full = lambda r, c: pl.BlockSpec((r, c), lambda i: (0, 0))
    cq, ckv, kr = pl.pallas_call(
        functools.partial(_mla_in_body, qr=qr, kvr=kvr),
        grid=(rows // tm,),
        in_specs=[row(h2.shape[1]), full(*w.shape), row(LANES), row(LANES), full(1, qr), full(1, kvr)],
        out_specs=[row(qr), row(kvr), row(LANES)],
        out_shape=[jax.ShapeDtypeStruct((rows, qr), BF16), jax.ShapeDtypeStruct((rows, kvr), BF16),
                   jax.ShapeDtypeStruct((rows, LANES), BF16)],
        compiler_params=_params("arbitrary"),
        name="mla_in",
    )(h2, w, cos, sin, q_norm[None, :], kv_norm[None, :])
    wq = w_q_up.reshape(qr, heads, MLA_NOPE + MLA_ROPE)
    wq = jnp.concatenate([wq[:, :, :MLA_NOPE].reshape(qr, -1), wq[:, :, MLA_NOPE:].reshape(qr, -1)], axis=1).astype(BF16)
    wkv = w_kv_up.reshape(kvr, heads, MLA_NOPE + MLA_V)
    wkv = jnp.concatenate([wkv[:, :, :MLA_NOPE].reshape(kvr, -1), wkv[:, :, MLA_NOPE:].reshape(kvr, -1)], axis=1).astype(BF16)
    scale = (MLA_NOPE + MLA_ROPE) ** -0.5 * LOG2E
    n_nope = heads * MLA_NOPE
    n_rope = heads * MLA_ROPE
    tn = _col_tile(n_nope, n_rope)
    q_nope, = _mm_call(functools.partial(_plain_body, scale=scale), cq, wq, col_blk0=0, n_blk=n_nope // tn,
                       tn=tn, tm=tm, out_dtypes=[BF16], name="mla_q_nope")
    gain = jnp.full((1, n_rope), scale, F32)
    q_rope, = _mm_call(functools.partial(_rope_body, swap=MLA_ROPE // 4, norm=False), cq, wq,
                       col_blk0=n_nope // tn, n_blk=n_rope // tn, tn=tn, tm=tm, out_dtypes=[BF16],
                       row_ins=(cos, sin), col_ins=(gain,), name="mla_q_rope")
    tn = _col_tile(wkv.shape[1])
    kv, = _mm_call(functools.partial(_plain_body, scale=1.0), ckv, wkv, col_blk0=0, n_blk=wkv.shape[1] // tn,
                   tn=tn, tm=tm, out_dtypes=[BF16], name="mla_kv_up")
    r3 = lambda a: a.reshape(b, t, -1)
    return _mla_attention(r3(q_nope), r3(q_rope), r3(kv), r3(kr), n_lat, heads)


def _hgrn_mixer(h2, w_in, lb_raw, out_gain, layer, b, t, n_lat):
    w = w_in
    width = w.shape[1] // 5
    tn, tm = _col_tile(width), MM_ROW_TILE
    nb = width // tn
    qs, = _mm_call(_silu_body, h2, w, col_blk0=0, n_blk=nb, tn=tn, tm=tm, out_dtypes=[BF16], name="hgrn_in_q")
    forget = functools.partial(_forget_body, layer=layer)
    kf, lff = _mm_call(forget, h2, w, col_blk0=nb, n_blk=nb, tn=tn, tm=tm, out_dtypes=[BF16, F32],
                       col_ins=(lb_raw,), name="hgrn_in_ff")
    kb, lfb = _mm_call(forget, h2, w, col_blk0=2 * nb, n_blk=nb, tn=tn, tm=tm, out_dtypes=[BF16, F32],
                       col_ins=(lb_raw,), name="hgrn_in_fb")
    ig, = _mm_call(functools.partial(_plain_body, scale=1.0), h2, w, col_blk0=3 * nb, n_blk=2 * nb, tn=tn, tm=tm,
                   out_dtypes=[BF16], name="hgrn_in_ig")
    r3 = lambda a: a.reshape(b, t, -1)
    return _gla(r3(qs), r3(kf), r3(lff), r3(kb), r3(lfb), r3(ig), out_gain, n_lat)


def kernel(x, c, ctx, c_ctx, ada_w, ada_b, ln_g, ln_b, ffn_w_in, ffn_w_out, ret_w_in, ret_decay_fwd, ret_decay_bwd, ret_w_out, gqa_w_in, gqa_q_norm, gqa_k_norm, gqa_w_out, mla_w_in, mla_q_norm, mla_w_q_up, mla_kv_norm, mla_w_kv_up, mla_w_out, hgrn_w_in, hgrn_lb_raw, hgrn_out_norm, hgrn_w_out):
    b, n_lat, d = x.shape
    n_ctx = ctx.shape[1]
    t = n_lat + n_ctx
    depth = ada_w.shape[0]
    alpha = (2 * depth) ** 0.25
    n_lat_tiles = n_lat // ROW_TILE
    n_tiles = t // ROW_TILE

    pad = (-(b + 1)) % 8
    cc = jnp.concatenate([c, c_ctx[None, :], jnp.zeros((pad, d), F32)], axis=0)
    mod = _ada_all(cc, ada_w, ada_b)

    def vec(layer, idx):
        m = mod[layer, :, idx * d:(idx + 1) * d]
        return jnp.stack([m[:b], jnp.broadcast_to(m[b], (b, d))], axis=1)[:, :, None, :]

    xs, h = _modulate0(x, ctx, vec(0, 0), vec(0, 1))

    for i in range(depth):
        m, j = i % N_MIXERS, i // N_MIXERS
        last = i == depth - 1
        n_out_tiles = n_lat_tiles if last else n_tiles
        h2 = h.reshape(b * t, d)
        if m == 0:
            a = _retention_mixer(h2, ret_w_in[j], ret_decay_fwd[j], ret_decay_bwd[j], b, t, n_lat)
            w_out = ret_w_out[j]
        elif m == 1:
            a = _gqa_mixer(h2, gqa_w_in[j], gqa_q_norm[j], gqa_k_norm[j], b, t, n_lat)
            w_out = gqa_w_out[j]
        elif m == 2:
            a = _mla_mixer(h2, mla_w_in[j], mla_q_norm[j], mla_w_q_up[j], mla_kv_norm[j], mla_w_kv_up[j], b, t, n_lat)
            w_out = mla_w_out[j]
        else:
            a = _hgrn_mixer(h2, hgrn_w_in[j], hgrn_lb_raw, hgrn_out_norm[j], i, b, t, n_lat)
            w_out = hgrn_w_out[j]
        xs, h = _outproj_ln(a, w_out.astype(BF16), xs, vec(i, 2), ln_g[i, 0][None, :], ln_b[i, 0][None, :],
                            (vec(i, 3), vec(i, 4)), alpha=alpha, n_lat_tiles=n_lat_tiles, n_tiles=n_out_tiles,
                            name="mixer_out_ln")
        rows_out = n_out_tiles * ROW_TILE
        act = _ffn_in(h.reshape(b * rows_out, d), ffn_w_in, i, MM_ROW_TILE, _col_tile(ffn_w_in.shape[2] // 2))
        act = act.reshape(b, rows_out, -1)
        if last:
            xs, = _outproj_ln(act, ffn_w_out[i].astype(BF16), xs, vec(i, 5), ln_g[i, 1][None, :], ln_b[i, 1][None, :],
                              (), alpha=alpha, n_lat_tiles=n_lat_tiles, n_tiles=n_lat_tiles, name="ffn_out_ln_final")
        else:
            xs, h = _outproj_ln(act, ffn_w_out[i].astype(BF16), xs, vec(i, 5), ln_g[i, 1][None, :], ln_b[i, 1][None, :],
                                (vec(i + 1, 0), vec(i + 1, 1)), alpha=alpha, n_lat_tiles=n_lat_tiles,
                                n_tiles=n_tiles, name="ffn_out_ln")
    return xs
```

```python
import functools

import jax
import jax.numpy as jnp
from jax import lax
from jax.experimental import pallas as pl
from jax.experimental.pallas import tpu as pltpu

F32 = jnp.float32
BF16 = jnp.bfloat16

GRID_W = 64
N_MIXERS = 4
RET_DK = 256
RET_DV = 512
RET_CHUNK = 128
GQA_HEAD_DIM = 128
GQA_GROUP = 4
MLA_NOPE = 128
MLA_ROPE = 64
MLA_V = 128
HGRN_DIM = 128
GLA_CHUNK = 128
ROPE_THETA = 10000.0
EPS = 1e-6

LANES = 128
ROW_TILE = 256
ATTN_KV_CHUNKS = (512, 256)
LOG2E = 1.4426950408889634
V7X_VMEM_LIMIT = 56 * 1024 * 1024


def _params(*sem):
    return pltpu.CompilerParams(dimension_semantics=sem, vmem_limit_bytes=V7X_VMEM_LIMIT)


def _dot(a, b):
    return jnp.dot(a, b, preferred_element_type=F32)


def _dot_nt(a, b):
    return lax.dot_general(a, b, (((1,), (1,)), ((), ())), preferred_element_type=F32)


def _dot_tn(a, b):
    return lax.dot_general(a, b, (((0,), (0,)), ((), ())), preferred_element_type=F32)


def _silu(x):
    return x * jax.nn.sigmoid(x)


def _swap_halves(x, w):
    if 2 * w == LANES:
        return pltpu.roll(x, w, axis=1)
    up = pltpu.roll(x, LANES - w, axis=1)
    down = pltpu.roll(x, w, axis=1)
    lane = lax.broadcasted_iota(jnp.int32, x.shape, 1)
    return jnp.where((lane % (2 * w)) < w, up, down)


def _ada_kernel(c_ref, w_ref, b_ref, o_ref):
    s = _silu(c_ref[...]).astype(BF16)
    o_ref[...] = _dot(s, w_ref[...].astype(BF16)) + b_ref[...]


def _ada_all(cc, ada_w, ada_b, tn=1024):
    depth, d, n = ada_w.shape
    rows = cc.shape[0]
    return pl.pallas_call(
        _ada_kernel,
        grid=(depth, n // tn),
        in_specs=[pl.BlockSpec((rows, d), lambda l, j: (0, 0)),
                  pl.BlockSpec((None, d, tn), lambda l, j: (l, 0, j)),
                  pl.BlockSpec((None, 1, tn), lambda l, j: (l, 0, j))],
        out_specs=pl.BlockSpec((None, rows, tn), lambda l, j: (l, 0, j)),
        out_shape=jax.ShapeDtypeStruct((depth, rows, n), F32),
        compiler_params=_params("arbitrary", "arbitrary"),
        name="ada_mod",
    )(cc, ada_w, ada_b.reshape(depth, 1, n))


def _mod_kernel(x_ref, ctx_ref, sh_ref, sc_ref, xs_ref, h_ref, *, n_lat_tiles):
    def emit(src_ref):
        xv = src_ref[...]
        xs_ref[...] = xv
        h_ref[...] = (xv * (1.0 + sc_ref[...]) + sh_ref[...]).astype(BF16)

    pl.when(pl.program_id(1) < n_lat_tiles)(lambda: emit(x_ref))
    pl.when(pl.program_id(1) >= n_lat_tiles)(lambda: emit(ctx_ref))


def _modulate0(x, ctx, sh, sc):
    b, n_lat, d = x.shape
    n_lat_tiles = n_lat // ROW_TILE
    n_tiles = n_lat_tiles + ctx.shape[1] // ROW_TILE
    vec = pl.BlockSpec((None, None, 1, d), lambda i, j: (i, j // n_lat_tiles, 0, 0))
    tile = pl.BlockSpec((None, ROW_TILE, d), lambda i, j: (i, j, 0))
    return pl.pallas_call(
        functools.partial(_mod_kernel, n_lat_tiles=n_lat_tiles),
        grid=(b, n_tiles),
        in_specs=[pl.BlockSpec((None, ROW_TILE, d), lambda i, j: (i, jnp.minimum(j, n_lat_tiles - 1), 0)),
                  pl.BlockSpec((None, ROW_TILE, d), lambda i, j: (i, jnp.maximum(j - n_lat_tiles, 0), 0)),
                  vec, vec],
        out_specs=[tile, tile],
        out_shape=[jax.ShapeDtypeStruct((b, n_tiles * ROW_TILE, d), F32),
                   jax.ShapeDtypeStruct((b, n_tiles * ROW_TILE, d), BF16)],
        compiler_params=_params("arbitrary", "arbitrary"),
        name="modulate0",
    )(x, ctx, sh, sc)


def _with_bf16_weights(body, n_w):
    def kern(h_ref, *refs):
        w_refs, rest, wbf_refs = refs[:n_w], refs[n_w:len(refs) - n_w], refs[len(refs) - n_w:]

        @pl.when(pl.program_id(1) == 0)
        def _():
            for w_ref, wbf_ref in zip(w_refs, wbf_refs):
                wbf_ref[...] = w_ref[...].astype(BF16)

        body(h_ref, *wbf_refs, *rest)
    return kern


def _mm_call(body, h, w, *, col_blk0, n_blk, tn, tm, out_dtypes, name, row_ins=(), col_ins=(), const_ins=()):
    rows, k = h.shape
    assert rows % tm == 0 and w.shape[0] == k
    in_specs = [pl.BlockSpec((tm, k), lambda j, i: (i, 0)),
                pl.BlockSpec((k, tn), lambda j, i: (0, j + col_blk0))]
    args = [h, w]
    for a in row_ins:
        in_specs.append(pl.BlockSpec((tm, a.shape[1]), lambda j, i: (i, 0)))
        args.append(a)
    for a in col_ins:
        assert a.shape[1] == n_blk * tn
        in_specs.append(pl.BlockSpec((a.shape[0], tn), lambda j, i: (0, j)))
        args.append(a)
    for a in const_ins:
        in_specs.append(pl.BlockSpec(a.shape, lambda j, i: (0, 0)))
        args.append(a)
    scratch = []
    if w.dtype != BF16:
        body = _with_bf16_weights(body, 1)
        scratch = [pltpu.VMEM((k, tn), BF16)]
    out = pl.pallas_call(
        body,
        grid=(n_blk, rows // tm),
        in_specs=in_specs,
        out_specs=[pl.BlockSpec((tm, tn), lambda j, i: (i, j)) for _ in out_dtypes],
        out_shape=[jax.ShapeDtypeStruct((rows, n_blk * tn), dt) for dt in out_dtypes],
        scratch_shapes=scratch,
        compiler_params=_params("arbitrary", "arbitrary"),
        name=name,
    )(*args)
    return out


def _plain_body(h_ref, w_ref, o_ref, *, scale):
    acc = _dot(h_ref[...], w_ref[...])
    if scale != 1.0:
        acc = acc * scale
    o_ref[...] = acc.astype(o_ref.dtype)


def _silu_body(h_ref, w_ref, o_ref):
    o_ref[...] = _silu(_dot(h_ref[...], w_ref[...])).astype(o_ref.dtype)


def _rope_body(h_ref, w_ref, cos_ref, sin_ref, gain_ref, *rest, swap, norm):
    o_ref = rest[-1]
    acc = _dot(h_ref[...], w_ref[...])
    tn = acc.shape[1]
    tw = cos_ref.shape[1]
    if norm:
        acc = acc * lax.rsqrt(_dot((acc * acc).astype(BF16), rest[0][...]) + EPS)
    for c in range(tn // LANES):
        sl = slice(c * LANES, (c + 1) * LANES)
        x = acc[:, sl] * gain_ref[:, sl]
        t = c % (tw // LANES)
        tsl = slice(t * LANES, (t + 1) * LANES)
        y = x * cos_ref[:, tsl] + _swap_halves(x, swap) * sin_ref[:, tsl]
        o_ref[:, sl] = y.astype(o_ref.dtype)


def _forget_body(h_ref, w_ref, raw_ref, k_ref, lf_ref, *, layer):
    raw = raw_ref[...]
    e = jnp.exp(raw - jnp.max(raw, axis=0, keepdims=True))
    p = e / jnp.sum(e, axis=0, keepdims=True)
    lb = jnp.zeros_like(p[0:1])
    for r in range(1, layer + 1):
        lb = lb + p[r:r + 1]
    f = lb + (1.0 - lb) * jax.nn.sigmoid(_dot(h_ref[...], w_ref[...]))
    k_ref[...] = (1.0 - f).astype(k_ref.dtype)
    lf_ref[...] = jnp.log(f)


def _mla_in_body(h_ref, w_ref, cos_ref, sin_ref, qn_ref, kvn_ref, cq_ref, ckv_ref, kr_ref, *, qr, kvr):
    acc = _dot(h_ref[...], w_ref[...])
    cq = acc[:, :qr]
    cq_ref[...] = (cq * lax.rsqrt(jnp.mean(cq * cq, axis=-1, keepdims=True) + EPS) * qn_ref[...]).astype(BF16)
    ckv = acc[:, qr:qr + kvr]
    ckv_ref[...] = (ckv * lax.rsqrt(jnp.mean(ckv * ckv, axis=-1, keepdims=True) + EPS) * kvn_ref[...]).astype(BF16)
    kr = acc[:, qr + kvr:]
    kr_ref[...] = (kr * cos_ref[...] + _swap_halves(kr, MLA_ROPE // 4) * sin_ref[...]).astype(BF16)


def _swiglu_body(h_ref, wa_ref, wb_ref, o_ref):
    h = h_ref[...]
    a = _dot(h, wa_ref[...])
    b = _dot(h, wb_ref[...])
    o_ref[...] = (_silu(a) * b).astype(o_ref.dtype)


def _ffn_in(h, w_all, layer, tm, tn):
    rows, k = h.shape
    hidden = w_all.shape[2] // 2
    nb = hidden // tn
    w = w_all
    return pl.pallas_call(
        _with_bf16_weights(_swiglu_body, 2),
        grid=(nb, rows // tm),
        in_specs=[pl.BlockSpec((tm, k), lambda j, i: (i, 0)),
                  pl.BlockSpec((None, k, tn), lambda j, i: (layer, 0, j)),
                  pl.BlockSpec((None, k, tn), lambda j, i: (layer, 0, j + nb))],
        out_specs=pl.BlockSpec((tm, tn), lambda j, i: (i, j)),
        out_shape=jax.ShapeDtypeStruct((rows, hidden), BF16),
        scratch_shapes=[pltpu.VMEM((k, tn), BF16), pltpu.VMEM((k, tn), BF16)],
        compiler_params=_params("arbitrary", "arbitrary"),
        name="ffn_in_swiglu",
    )(h, w, w)


def _outproj_ln_kernel(*refs, alpha, n_lat_tiles, split):
    if split:
        a_ref, ac_ref, w_ref, x_ref, gate_ref, lng_ref, lnb_ref, *rest = refs
        a = jnp.where(pl.program_id(1) >= n_lat_tiles, ac_ref[...], a_ref[...])
    else:
        a_ref, w_ref, x_ref, gate_ref, lng_ref, lnb_ref, *rest = refs
        a = a_ref[...]
    y = _dot(a, w_ref[...])
    z = alpha * x_ref[...] + gate_ref[...] * y
    mean = jnp.mean(z, axis=-1, keepdims=True)
    var = jnp.maximum(jnp.mean(z * z, axis=-1, keepdims=True) - mean * mean, 0.0)
    t = (z - mean) * lax.rsqrt(var + EPS)
    g, b = lng_ref[...], lnb_ref[...]
    if len(rest) == 1:
        rest[0][...] = t * g + b
    else:
        sh_ref, sc_ref, xo_ref, ho_ref = rest
        xo_ref[...] = t * g + b
        sc1 = 1.0 + sc_ref[...]
        ho_ref[...] = (t * (g * sc1) + (b * sc1 + sh_ref[...])).astype(BF16)


def _outproj_ln(a, w, x, gate, lng, lnb, next_mod, *, alpha, n_lat_tiles, n_tiles, name):
    split = isinstance(a, tuple)
    a_args = a if split else (a,)
    b, _, k = a_args[0].shape
    d = w.shape[1]
    tile = lambda width: pl.BlockSpec((None, ROW_TILE, width), lambda i, j: (i, j, 0))
    if split:
        a_specs = [pl.BlockSpec((None, ROW_TILE, k), lambda i, j: (i, jnp.minimum(j, n_lat_tiles - 1), 0)),
                   pl.BlockSpec((None, ROW_TILE, k), lambda i, j: (i, jnp.maximum(j - n_lat_tiles, 0), 0))]
    else:
        a_specs = [tile(k)]
    vec = pl.BlockSpec((None, None, 1, d), lambda i, j: (i, j // n_lat_tiles, 0, 0))
    full = pl.BlockSpec((1, d), lambda i, j: (0, 0))
    out_shape = [jax.ShapeDtypeStruct((b, n_tiles * ROW_TILE, d), F32)]
    if next_mod:
        out_shape.append(jax.ShapeDtypeStruct((b, n_tiles * ROW_TILE, d), BF16))
    return pl.pallas_call(
        functools.partial(_outproj_ln_kernel, alpha=alpha, n_lat_tiles=n_lat_tiles, split=split),
        grid=(b, n_tiles),
        in_specs=a_specs + [pl.BlockSpec((k, d), lambda i, j: (0, 0), pipeline_mode=pl.Buffered(1)),
                            tile(d), vec, full, full] + [vec for _ in next_mod],
        out_specs=[tile(d) for _ in out_shape],
        out_shape=out_shape,
        compiler_params=_params("arbitrary", "arbitrary"),
        name=name,
    )(*a_args, w, x, gate, lng, lnb, *next_mod)


def _kv_chunk(n_lat):
    return next(ck for ck in ATTN_KV_CHUNKS if n_lat % ck == 0)


def _fill_vt(v, vt_ref, vtc_ref, n_lat, ck):
    dv = v.shape[1]
    vt = v.astype(F32).T.astype(BF16)
    for c in range(n_lat // ck):
        vt_ref[c, 0:dv, :] = vt[:, c * ck:(c + 1) * ck]
        vt_ref[c, dv:, :] = jnp.ones((vt_ref.shape[1] - dv, ck), BF16)
    vtc_ref[0:dv, :] = vt[:, n_lat:]
    vtc_ref[dv:, :] = jnp.ones((vtc_ref.shape[0] - dv, vtc_ref.shape[1]), BF16)


def _attend(streams, n_chunks, m_ref, acc_ref, sa_ref, sb_ref):
    assert n_chunks % 2 == 0
    dv = acc_ref.shape[1] // 2
    m_ref[...] = jnp.full(m_ref.shape, -jnp.inf, F32)
    acc_ref[...] = jnp.zeros(acc_ref.shape, F32)

    def accumulate(h, s, vtc):
        m_old = m_ref[h]
        m_new = jnp.maximum(m_old, jnp.max(s, axis=0, keepdims=True))
        p = jnp.exp2(s - m_new).astype(BF16)
        acc_ref[h] = jnp.exp2(m_old - m_new) * acc_ref[h] + _dot(vtc, p)
        m_ref[h] = m_new

    n_ctx = streams[0][2].shape[0]

    def stage(c_next, buf_next, c_cur, buf_cur):
        for h, (qs, k_lat, k_ctx, vt_ref, _) in enumerate(streams):
            if c_next == "ctx":
                buf_next[h, 0:n_ctx, :] = _dot_nt(k_ctx, qs)
            elif c_next is not None:
                buf_next[h] = _dot_nt(k_lat(c_next), qs)
            if c_cur is not None:
                accumulate(h, buf_cur[h], vt_ref[c_cur])

    stage(0, sa_ref, None, None)

    def pair(i, carry):
        stage(2 * i + 1, sb_ref, 2 * i, sa_ref)
        stage(2 * i + 2, sa_ref, 2 * i + 1, sb_ref)
        return carry

    lax.fori_loop(0, n_chunks // 2 - 1, pair, 0)
    stage(n_chunks - 1, sb_ref, n_chunks - 2, sa_ref)
    stage("ctx", sa_ref, n_chunks - 1, sb_ref)

    outs = []
    for h, (_, _, _, _, vtc_ref) in enumerate(streams):
        accumulate(h, sa_ref[h, 0:n_ctx, :], vtc_ref[...])
        acc = acc_ref[h]
        outs.append(acc[:dv] / acc[dv:])
    return outs


def _attend_ctx(qs, k_ctx, vtc):
    dv = vtc.shape[0] // 2
    s = _dot_nt(k_ctx, qs)
    p = jnp.exp2(s - jnp.max(s, axis=0, keepdims=True)).astype(BF16)
    acc = _dot(vtc, p)
    return acc[:dv] / acc[dv:]


GQA_KV_HEADS_PER_STEP = 2


ATTN_Q_TILE = 512


def _gqa_attn_kernel(q_ref, qc_ref, k_ref, v_ref, o_ref, oc_ref, vt_ref, vtc_ref, m_ref, acc_ref, sa_ref, sb_ref, *,
                     n_lat, ck):
    d = GQA_HEAD_DIM
    n_kv = k_ref.shape[1] // d
    n_chunks = n_lat // ck

    @pl.when(pl.program_id(2) == 0)
    def _():
        for s in range(n_kv):
            _fill_vt(v_ref[:, s * d:(s + 1) * d], vt_ref.at[pl.ds(s * n_chunks, n_chunks)], vtc_ref.at[s], n_lat, ck)

    def stacked(ref, s):
        return jnp.concatenate([ref[:, (s * GQA_GROUP + g) * d:(s * GQA_GROUP + g + 1) * d]
                                for g in range(GQA_GROUP)], axis=0)

    def emit(ref, s, o_t):
        tq = ref.shape[0]
        for g in range(GQA_GROUP):
            col = (s * GQA_GROUP + g) * d
            ref[:, col:col + d] = o_t[:, g * tq:(g + 1) * tq].T.astype(ref.dtype)

    streams = []
    for s in range(n_kv):
        ksl = slice(s * d, (s + 1) * d)
        k_lat = lambda c, ksl=ksl: k_ref[pl.ds(pl.multiple_of(c * ck, ck), ck), ksl]
        streams.append((stacked(q_ref, s), k_lat, k_ref[n_lat:, ksl],
                        vt_ref.at[pl.ds(s * n_chunks, n_chunks)], vtc_ref.at[s]))
    for s, o_t in enumerate(_attend(streams, n_chunks, m_ref, acc_ref, sa_ref, sb_ref)):
        emit(o_ref, s, o_t)

    @pl.when(pl.program_id(2) == pl.num_programs(2) - 1)
    def _():
        for s in range(n_kv):
            emit(oc_ref, s, _attend_ctx(stacked(qc_ref, s), k_ref[n_lat:, s * d:(s + 1) * d], vtc_ref[s]))


def _gqa_attention(qk, v, n_lat):
    b, t, _ = qk.shape
    d = GQA_HEAD_DIM
    hkv = v.shape[2] // d
    hq = hkv * GQA_GROUP
    n_kv = GQA_KV_HEADS_PER_STEP if hkv % GQA_KV_HEADS_PER_STEP == 0 else 1
    gw = n_kv * GQA_GROUP * d
    ck = _kv_chunk(n_lat)
    n_ctx = t - n_lat
    tq = ATTN_Q_TILE
    rows = GQA_GROUP * tq
    return pl.pallas_call(
        functools.partial(_gqa_attn_kernel, n_lat=n_lat, ck=ck),
        grid=(b, hkv // n_kv, n_lat // tq),
        in_specs=[pl.BlockSpec((None, tq, gw), lambda i, h, j: (i, j, h)),
                  pl.BlockSpec((None, n_ctx, gw), lambda i, h, j: (i, n_lat // n_ctx, h)),
                  pl.BlockSpec((None, t, n_kv * d), lambda i, h, j: (i, 0, hq // n_kv + h)),
                  pl.BlockSpec((None, t, n_kv * d), lambda i, h, j: (i, 0, h))],
        out_specs=[pl.BlockSpec((None, tq, gw), lambda i, h, j: (i, j, h)),
                   pl.BlockSpec((None, n_ctx, gw), lambda i, h, j: (i, 0, h))],
        out_shape=[jax.ShapeDtypeStruct((b, n_lat, hq * d), BF16), jax.ShapeDtypeStruct((b, n_ctx, hq * d), BF16)],
        scratch_shapes=[pltpu.VMEM((n_kv * (n_lat // ck), 2 * d, ck), BF16),
                        pltpu.VMEM((n_kv, 2 * d, n_ctx), BF16),
                        pltpu.VMEM((n_kv, 1, rows), F32), pltpu.VMEM((n_kv, 2 * d, rows), F32),
                        pltpu.VMEM((n_kv, ck, rows), F32), pltpu.VMEM((n_kv, ck, rows), F32)],
        compiler_params=_params("arbitrary", "arbitrary", "arbitrary"),
        name="gqa_attention",
    )(qk, qk, qk, v)


MLA_HEAD_GROUP = 4


def _mla_attn_kernel(qn_ref, qr_ref, qnc_ref, qrc_ref, kn_ref, kr_ref, v_ref, o_ref, oc_ref,
                     vt_ref, vtc_ref, m_ref, acc_ref, sa_ref, sb_ref, *, n_lat, ck):
    n_chunks = n_lat // ck

    @pl.when(pl.program_id(2) == 0)
    def _():
        for hh in range(MLA_HEAD_GROUP):
            _fill_vt(v_ref[:, hh * MLA_V:(hh + 1) * MLA_V], vt_ref.at[pl.ds(hh * n_chunks, n_chunks)],
                     vtc_ref.at[hh], n_lat, ck)

    def query(nope_ref, rope_ref, hh):
        pair = rope_ref[:, (hh // 2) * LANES:(hh // 2 + 1) * LANES]
        lane = lax.broadcasted_iota(jnp.int32, pair.shape, 1)
        qr = jnp.where((lane // MLA_ROPE) == (hh % 2), pair, jnp.zeros_like(pair))
        return jnp.concatenate([nope_ref[:, hh * MLA_NOPE:(hh + 1) * MLA_NOPE], qr], axis=1)

    def ctx_keys(hh):
        return jnp.concatenate([kn_ref[n_lat:, hh * MLA_NOPE:(hh + 1) * MLA_NOPE], kr_ref[n_lat:, :]], axis=1)

    streams = []
    for hh in range(MLA_HEAD_GROUP):
        def k_lat(c, hh=hh):
            sl = pl.ds(pl.multiple_of(c * ck, ck), ck)
            return jnp.concatenate([kn_ref[sl, hh * MLA_NOPE:(hh + 1) * MLA_NOPE], kr_ref[sl, :]], axis=1)

        streams.append((query(qn_ref, qr_ref, hh), k_lat, ctx_keys(hh),
                        vt_ref.at[pl.ds(hh * n_chunks, n_chunks)], vtc_ref.at[hh]))
    for hh, o_t in enumerate(_attend(streams, n_chunks, m_ref, acc_ref, sa_ref, sb_ref)):
        o_ref[:, hh * MLA_V:(hh + 1) * MLA_V] = o_t.T.astype(o_ref.dtype)

    @pl.when(pl.program_id(2) == pl.num_programs(2) - 1)
    def _():
        for hh in range(MLA_HEAD_GROUP):
            o_t = _attend_ctx(query(qnc_ref, qrc_ref, hh), ctx_keys(hh), vtc_ref[hh])
            oc_ref[:, hh * MLA_V:(hh + 1) * MLA_V] = o_t.T.astype(oc_ref.dtype)


def _mla_attention(q_nope, q_rope, kv, kr, n_lat, heads):
    b, t, _ = q_nope.shape
    hg = MLA_HEAD_GROUP
    ck = _kv_chunk(n_lat)
    n_groups = heads // hg
    rope_w = hg * MLA_ROPE
    n_ctx = t - n_lat
    tq = ATTN_Q_TILE
    ctx_blk = n_lat // n_ctx
    return pl.pallas_call(
        functools.partial(_mla_attn_kernel, n_lat=n_lat, ck=ck),
        grid=(b, n_groups, n_lat // tq),
        in_specs=[pl.BlockSpec((None, tq, hg * MLA_NOPE), lambda i, h, j: (i, j, h)),
                  pl.BlockSpec((None, tq, rope_w), lambda i, h, j: (i, j, h)),
                  pl.BlockSpec((None, n_ctx, hg * MLA_NOPE), lambda i, h, j: (i, ctx_blk, h)),
                  pl.BlockSpec((None, n_ctx, rope_w), lambda i, h, j: (i, ctx_blk, h)),
                  pl.BlockSpec((None, t, hg * MLA_NOPE), lambda i, h, j: (i, 0, h)),
                  pl.BlockSpec((None, t, LANES), lambda i, h, j: (i, 0, 0)),
                  pl.BlockSpec((None, t, hg * MLA_V), lambda i, h, j: (i, 0, n_groups + h))],
        out_specs=[pl.BlockSpec((None, tq, hg * MLA_V), lambda i, h, j: (i, j, h)),
                   pl.BlockSpec((None, n_ctx, hg * MLA_V), lambda i, h, j: (i, 0, h))],
        out_shape=[jax.ShapeDtypeStruct((b, n_lat, heads * MLA_V), BF16),
                   jax.ShapeDtypeStruct((b, n_ctx, heads * MLA_V), BF16)],
        scratch_shapes=[pltpu.VMEM((hg * (n_lat // ck), 2 * MLA_V, ck), BF16),
                        pltpu.VMEM((hg, 2 * MLA_V, n_ctx), BF16),
                        pltpu.VMEM((hg, 1, tq), F32), pltpu.VMEM((hg, 2 * MLA_V, tq), F32),
                        pltpu.VMEM((hg, ck, tq), F32), pltpu.VMEM((hg, ck, tq), F32)],
        compiler_params=_params("arbitrary", "arbitrary", "arbitrary"),
        name="mla_attention",
    )(q_nope, q_rope, q_nope, q_rope, kv, kr, kv)


def _chunk_order(step, n_lat_chunks, n_chunks, reverse):
    if reverse:
        return n_chunks - 1 - step
    n_ctx = n_chunks - n_lat_chunks
    return jnp.where(step < n_ctx, n_lat_chunks + step, step - n_ctx)


def _ret_kernel(lg_ref, q_ref, k_ref, v_ref, *rest, reverse, heads):
    if reverse:
        of_ref, g_ref, o_ref, s_ref, mask_ref, qd_ref, kd_ref = rest
    else:
        o_ref, s_ref, mask_ref, qd_ref, kd_ref = rest
    c = RET_CHUNK

    @pl.when(pl.program_id(1) == 0)
    def _():
        s_ref[...] = jnp.zeros(s_ref.shape, F32)
        i = lax.broadcasted_iota(jnp.int32, (c, c), 0)
        j = lax.broadcasted_iota(jnp.int32, (c, c), 1)
        diff = (j - i) if reverse else (i - j)
        pos = lax.broadcasted_iota(jnp.int32, (c, RET_DK), 0)
        pos = (c - 1 - pos) if reverse else pos
        for h in range(heads):
            lg = lg_ref[h]
            mask_ref[h] = jnp.where(diff >= 0, jnp.exp(jnp.maximum(diff, 0).astype(F32) * lg), 0.0)
            qd_ref[h] = jnp.exp((pos + 1).astype(F32) * lg)
            kd_ref[h] = jnp.exp((c - 1 - pos).astype(F32) * lg)

    for h in range(heads):
        ksl = slice(h * RET_DK, (h + 1) * RET_DK)
        vsl = slice(h * RET_DV, (h + 1) * RET_DV)
        q = q_ref[:, ksl]
        k = k_ref[:, ksl]
        v = v_ref[:, vsl]
        s_old = s_ref[h]
        scores = _dot_nt(q, k) * mask_ref[h]
        o = _dot(scores.astype(BF16), v) + _dot((q.astype(F32) * qd_ref[h]).astype(BF16), s_old.astype(BF16))
        chunk_decay = jnp.exp(jnp.zeros((1, RET_DV), F32) + float(c) * lg_ref[h])
        s_ref[h] = s_old * chunk_decay + _dot_tn((k.astype(F32) * kd_ref[h]).astype(BF16), v)
        if reverse:
            o = o + of_ref[:, vsl]
            o = o * lax.rsqrt(jnp.mean(o * o, axis=-1, keepdims=True) + EPS)
            o_ref[:, vsl] = (o * _silu(g_ref[:, vsl].astype(F32))).astype(o_ref.dtype)
        else:
            o_ref[:, vsl] = o


def _retention(qk, vg, lg_fwd, lg_bwd, n_lat):
    b, t, _ = qk.shape
    heads = qk.shape[2] // (2 * RET_DK)
    c = RET_CHUNK
    n_chunks, n_lat_chunks = t // c, n_lat // c
    wk, wv = heads * RET_DK, heads * RET_DV

    def call(reverse, lg, extra_args, extra_specs, out_dtype):
        order = functools.partial(_chunk_order, n_lat_chunks=n_lat_chunks, n_chunks=n_chunks, reverse=reverse)
        blk = lambda width, col: pl.BlockSpec((None, c, width), lambda i, s: (i, order(s), col))
        return pl.pallas_call(
            functools.partial(_ret_kernel, reverse=reverse, heads=heads),
            grid=(b, n_chunks),
            in_specs=[pl.BlockSpec(memory_space=pltpu.SMEM), blk(wk, 0), blk(wk, 1), blk(wv, 0)]
                     + [blk(wv, col) for col in extra_specs],
            out_specs=blk(wv, 0),
            out_shape=jax.ShapeDtypeStruct((b, t, wv), out_dtype),
            scratch_shapes=[pltpu.VMEM((heads, RET_DK, RET_DV), F32), pltpu.VMEM((heads, c, c), F32),
                            pltpu.VMEM((heads, c, RET_DK), F32), pltpu.VMEM((heads, c, RET_DK), F32)],
            compiler_params=_params("arbitrary", "arbitrary"),
            name="retention_bwd" if reverse else "retention_fwd",
        )(lg, qk, qk, vg, *extra_args)

    o_f = call(False, lg_fwd, (), (), F32)
    return call(True, lg_bwd, (o_f, vg), (0, 1), BF16)


GLA_BATCH_GROUP = 2


def _gla_kernel(q_ref, k_ref, lf_ref, v_ref, *rest, reverse, heads):
    if reverse:
        of_ref, g_ref, gain_ref, o_ref, st_ref = rest
    else:
        o_ref, st_ref = rest
    c = GLA_CHUNK
    d = HGRN_DIM

    @pl.when(pl.program_id(1) == 0)
    def _():
        st_ref[...] = jnp.zeros(st_ref.shape, F32)

    i = lax.broadcasted_iota(jnp.int32, (c, c), 0)
    j = lax.broadcasted_iota(jnp.int32, (c, c), 1)
    tri = (j >= i) if reverse else (i >= j)
    tri_bf = jnp.where(tri, 1.0, 0.0).astype(BF16)
    mid = c // 2 if reverse else c // 2 - 1
    for n in range(q_ref.shape[0]):
        lf = lf_ref[n]
        lf_hi = lf.astype(BF16)
        lf_lo = (lf - lf_hi.astype(F32)).astype(BF16)
        bcum = _dot(tri_bf, lf_hi) + _dot(tri_bf, lf_lo)
        b_last = bcum[0:1, :] if reverse else bcum[c - 1:c, :]
        b_mid = bcum[mid:mid + 1, :]
        qf = q_ref[n].astype(F32)
        kf = k_ref[n].astype(F32)
        q_in = (qf * jnp.exp(bcum - b_mid)).astype(BF16)
        k_in = (kf * jnp.exp(b_mid - bcum)).astype(BF16)
        q_st = (qf * jnp.exp(bcum)).astype(BF16)
        k_out = (kf * jnp.exp(b_last - bcum)).astype(BF16)
        decay = jnp.exp(b_last)
        v = v_ref[n]
        for h in range(heads):
            sl = slice(h * d, (h + 1) * d)
            st = st_ref[n, h]
            scores = jnp.where(tri, _dot_nt(q_in[:, sl], k_in[:, sl]), 0.0)
            o = _dot_nt(q_st[:, sl], st.astype(BF16)) + _dot(scores.astype(BF16), v[:, sl])
            st_ref[n, h] = st * decay[:, sl] + _dot_tn(v[:, sl], k_out[:, sl])
            if reverse:
                o = o + of_ref[n, :, sl]
                o = o * lax.rsqrt(jnp.mean(o * o, axis=-1, keepdims=True) + EPS) * gain_ref[...]
                o_ref[n, :, sl] = (o * _silu(g_ref[n, :, sl].astype(F32))).astype(o_ref.dtype)
            else:
                o_ref[n, :, sl] = o


def _gla(qs, kf, lff, kb, lfb, ig, gain, n_lat, n_out_tiles=None):
    b, t, w = qs.shape
    heads = w // HGRN_DIM
    c = GLA_CHUNK
    n_chunks, n_lat_chunks = t // c, n_lat // c

    bg = GLA_BATCH_GROUP if b % GLA_BATCH_GROUP == 0 else 1

    def call(reverse, k, lf, extra_args, extra_specs, out_dtype):
        order = functools.partial(_chunk_order, n_lat_chunks=n_lat_chunks, n_chunks=n_chunks, reverse=reverse)
        blk = lambda col: pl.BlockSpec((bg, c, w), lambda i, s: (i, order(s), col))
        return pl.pallas_call(
            functools.partial(_gla_kernel, reverse=reverse, heads=heads),
            grid=(b // bg, n_chunks),
            in_specs=[blk(0), blk(0), blk(0), blk(0)] + [spec(blk) for spec in extra_specs],
            out_specs=blk(0),
            out_shape=jax.ShapeDtypeStruct((b, t, w), out_dtype),
            scratch_shapes=[pltpu.VMEM((bg, heads, HGRN_DIM, HGRN_DIM), F32)],
            compiler_params=_params("arbitrary", "arbitrary"),
            name="gla_bwd" if reverse else "gla_fwd",
        )(qs, k, lf, ig, *extra_args)

    o_f = call(False, kf, lff, (), (), F32)
    gain_spec = lambda blk: pl.BlockSpec((1, HGRN_DIM), lambda i, s: (0, 0))
    return call(True, kb, lfb, (o_f, ig, gain.reshape(1, HGRN_DIM)),
                (lambda blk: blk(0), lambda blk: blk(1), gain_spec), BF16)


def _rope_tables(rot_dim, n_lat, n_ctx, batch, width):
    half = rot_dim // 2
    freqs = ROPE_THETA ** (-jnp.arange(0, half, 2, dtype=F32) / half)
    tok = jnp.arange(n_lat)
    a_row = (tok // GRID_W).astype(F32)[:, None] * freqs
    a_col = (tok % GRID_W).astype(F32)[:, None] * freqs
    cos = jnp.concatenate([jnp.cos(a_row), jnp.cos(a_row), jnp.cos(a_col), jnp.cos(a_col)], axis=-1)
    sin = jnp.concatenate([-jnp.sin(a_row), jnp.sin(a_row), -jnp.sin(a_col), jnp.sin(a_col)], axis=-1)
    cos = jnp.concatenate([cos, jnp.ones((n_ctx, rot_dim), F32)], axis=0)
    sin = jnp.concatenate([sin, jnp.zeros((n_ctx, rot_dim), F32)], axis=0)
    reps = (batch, width // rot_dim)
    return jnp.tile(cos, reps), jnp.tile(sin, reps)


MM_ROW_TILE = 1024
MM_COL_TILES = (1024, 512, 256, 128)


def _col_tile(*widths):
    return next(tn for tn in MM_COL_TILES if all(w % tn == 0 for w in widths))


def _retention_mixer(h2, w_in, lg_fwd, lg_bwd, b, t, n_lat):
    heads = lg_fwd.shape[0]
    hk = heads * RET_DK
    w = w_in
    cos, sin = _rope_tables(RET_DK, n_lat, t - n_lat, b, RET_DK)
    gain = jnp.concatenate([jnp.full((1, hk), RET_DK ** -0.5, F32), jnp.ones((1, hk), F32)], axis=1)
    tn = _col_tile(2 * hk, w.shape[1] - 2 * hk)
    qk, = _mm_call(functools.partial(_rope_body, swap=RET_DK // 4, norm=False), h2, w,
                   col_blk0=0, n_blk=2 * hk // tn, tn=tn, tm=MM_ROW_TILE, out_dtypes=[BF16],
                   row_ins=(cos, sin), col_ins=(gain,), name="ret_in_qk")
    vg, = _mm_call(functools.partial(_plain_body, scale=1.0), h2, w,
                   col_blk0=2 * hk // tn, n_blk=(w.shape[1] - 2 * hk) // tn, tn=tn, tm=MM_ROW_TILE,
                   out_dtypes=[BF16], name="ret_in_vg")
    return _retention(qk.reshape(b, t, -1), vg.reshape(b, t, -1), lg_fwd, lg_bwd, n_lat)


def _gqa_mixer(h2, w_in, q_gain, k_gain, b, t, n_lat):
    d = GQA_HEAD_DIM
    w = w_in
    n_qk = w.shape[1] * (GQA_GROUP + 1) // (GQA_GROUP + 2)
    hq = n_qk // d * GQA_GROUP // (GQA_GROUP + 1)
    cos, sin = _rope_tables(d, n_lat, t - n_lat, b, d)
    gain = jnp.concatenate([jnp.tile(q_gain * (d ** -0.5 * LOG2E), hq), jnp.tile(k_gain, n_qk // d - hq)])[None, :]
    tn = _col_tile(n_qk, w.shape[1] - n_qk)
    head_of = jnp.arange(tn) // d
    block_mean = jnp.where(head_of[:, None] == head_of[None, :], 1.0 / d, 0.0).astype(BF16)
    qk, = _mm_call(functools.partial(_rope_body, swap=d // 4, norm=True), h2, w,
                   col_blk0=0, n_blk=n_qk // tn, tn=tn, tm=MM_ROW_TILE, out_dtypes=[BF16],
                   row_ins=(cos, sin), col_ins=(gain.astype(F32),), const_ins=(block_mean,), name="gqa_in_qk")
    v, = _mm_call(functools.partial(_plain_body, scale=1.0), h2, w,
                  col_blk0=n_qk // tn, n_blk=(w.shape[1] - n_qk) // tn, tn=tn, tm=MM_ROW_TILE,
                  out_dtypes=[BF16], name="gqa_in_v")
    return tuple(_gqa_attention(qk.reshape(b, t, -1), v.reshape(b, t, -1), n_lat))


def _mla_mixer(h2, w_in, q_norm, w_q_up, kv_norm, w_kv_up, b, t, n_lat):
    qr, kvr = q_norm.shape[0], kv_norm.shape[0]
    heads = w_q_up.shape[1] // (MLA_NOPE + MLA_ROPE)
    rows = h2.shape[0]
    w = jnp.concatenate([w_in, w_in[:, qr + kvr:]], axis=1).astype(BF16)
    cos, sin = _rope_tables(MLA_ROPE, n_lat, t - n_lat, b, LANES)
    tm = MM_ROW_TILE
    row = lambda width: pl.BlockSpec((tm, width), lambda i: (i, 0))
    full = lambda r, c: pl.BlockSpec((r, c), lambda i: (0, 0))
    cq, ckv, kr = pl.pallas_call(
        functools.partial(_mla_in_body, qr=qr, kvr=kvr),
        grid=(rows // tm,),
        in_specs=[row(h2.shape[1]), full(*w.shape), row(LANES), row(LANES), full(1, qr), full(1, kvr)],
        out_specs=[row(qr), row(kvr), row(LANES)],
        out_shape=[jax.ShapeDtypeStruct((rows, qr), BF16), jax.ShapeDtypeStruct((rows, kvr), BF16),
                   jax.ShapeDtypeStruct((rows, LANES), BF16)],
        compiler_params=_params("arbitrary"),
        name="mla_in",
    )(h2, w, cos, sin, q_norm[None, :], kv_norm[None, :])
    wq = w_q_up.reshape(qr, heads, MLA_NOPE + MLA_ROPE)
    wq = jnp.concatenate([wq[:, :, :MLA_NOPE].reshape(qr, -1), wq[:, :, MLA_NOPE:].reshape(qr, -1)], axis=1).astype(BF16)
    wkv = w_kv_up.reshape(kvr, heads, MLA_NOPE + MLA_V)
    wkv = jnp.concatenate([wkv[:, :, :MLA_NOPE].reshape(kvr, -1), wkv[:, :, MLA_NOPE:].reshape(kvr, -1)], axis=1).astype(BF16)
    scale = (MLA_NOPE + MLA_ROPE) ** -0.5 * LOG2E
    n_nope = heads * MLA_NOPE
    n_rope = heads * MLA_ROPE
    tn = _col_tile(n_nope, n_rope)
    q_nope, = _mm_call(functools.partial(_plain_body, scale=scale), cq, wq, col_blk0=0, n_blk=n_nope // tn,
                       tn=tn, tm=tm, out_dtypes=[BF16], name="mla_q_nope")
    gain = jnp.full((1, n_rope), scale, F32)
    q_rope, = _mm_call(functools.partial(_rope_body, swap=MLA_ROPE // 4, norm=False), cq, wq,
                       col_blk0=n_nope // tn, n_blk=n_rope // tn, tn=tn, tm=tm, out_dtypes=[BF16],
                       row_ins=(cos, sin), col_ins=(gain,), name="mla_q_rope")
    tn = _col_tile(wkv.shape[1])
    kv, = _mm_call(functools.partial(_plain_body, scale=1.0), ckv, wkv, col_blk0=0, n_blk=wkv.shape[1] // tn,
                   tn=tn, tm=tm, out_dtypes=[BF16], name="mla_kv_up")
    r3 = lambda a: a.reshape(b, t, -1)
    return tuple(_mla_attention(r3(q_nope), r3(q_rope), r3(kv), r3(kr), n_lat, heads))


def _hgrn_mixer(h2, w_in, lb_raw, out_gain, layer, b, t, n_lat):
    w = w_in
    width = w.shape[1] // 5
    tn, tm = _col_tile(width), MM_ROW_TILE
    nb = width // tn
    qs, = _mm_call(_silu_body, h2, w, col_blk0=0, n_blk=nb, tn=tn, tm=tm, out_dtypes=[BF16], name="hgrn_in_q")
    forget = functools.partial(_forget_body, layer=layer)
    kf, lff = _mm_call(forget, h2, w, col_blk0=nb, n_blk=nb, tn=tn, tm=tm, out_dtypes=[BF16, F32],
                       col_ins=(lb_raw,), name="hgrn_in_ff")
    kb, lfb = _mm_call(forget, h2, w, col_blk0=2 * nb, n_blk=nb, tn=tn, tm=tm, out_dtypes=[BF16, F32],
                       col_ins=(lb_raw,), name="hgrn_in_fb")
    ig, = _mm_call(functools.partial(_plain_body, scale=1.0), h2, w, col_blk0=3 * nb, n_blk=2 * nb, tn=tn, tm=tm,
                   out_dtypes=[BF16], name="hgrn_in_ig")
    r3 = lambda a: a.reshape(b, t, -1)
    return _gla(r3(qs), r3(kf), r3(lff), r3(kb), r3(lfb), r3(ig), out_gain, n_lat)


def kernel(x, c, ctx, c_ctx, ada_w, ada_b, ln_g, ln_b, ffn_w_in, ffn_w_out, ret_w_in, ret_decay_fwd, ret_decay_bwd, ret_w_out, gqa_w_in, gqa_q_norm, gqa_k_norm, gqa_w_out, mla_w_in, mla_q_norm, mla_w_q_up, mla_kv_norm, mla_w_kv_up, mla_w_out, hgrn_w_in, hgrn_lb_raw, hgrn_out_norm, hgrn_w_out):
    b, n_lat, d = x.shape
    n_ctx = ctx.shape[1]
    t = n_lat + n_ctx
    depth = ada_w.shape[0]
    alpha = (2 * depth) ** 0.25
    n_lat_tiles = n_lat // ROW_TILE
    n_tiles = t // ROW_TILE

    pad = (-(b + 1)) % 8
    cc = jnp.concatenate([c, c_ctx[None, :], jnp.zeros((pad, d), F32)], axis=0)
    mod = _ada_all(cc, ada_w, ada_b)

    def vec(layer, idx):
        m = mod[layer, :, idx * d:(idx + 1) * d]
        return jnp.stack([m[:b], jnp.broadcast_to(m[b], (b, d))], axis=1)[:, :, None, :]

    xs, h = _modulate0(x, ctx, vec(0, 0), vec(0, 1))

    for i in range(depth):
        m, j = i % N_MIXERS, i // N_MIXERS
        last = i == depth - 1
        n_out_tiles = n_lat_tiles if last else n_tiles
        h2 = h.reshape(b * t, d)
        if m == 0:
            a = _retention_mixer(h2, ret_w_in[j], ret_decay_fwd[j], ret_decay_bwd[j], b, t, n_lat)
            w_out = ret_w_out[j]
        elif m == 1:
            a = _gqa_mixer(h2, gqa_w_in[j], gqa_q_norm[j], gqa_k_norm[j], b, t, n_lat)
            w_out = gqa_w_out[j]
        elif m == 2:
            a = _mla_mixer(h2, mla_w_in[j], mla_q_norm[j], mla_w_q_up[j], mla_kv_norm[j], mla_w_kv_up[j], b, t, n_lat)
            w_out = mla_w_out[j]
        else:
            a = _hgrn_mixer(h2, hgrn_w_in[j], hgrn_lb_raw, hgrn_out_norm[j], i, b, t, n_lat)
            w_out = hgrn_w_out[j]
        xs, h = _outproj_ln(a, w_out.astype(BF16), xs, vec(i, 2), ln_g[i, 0][None, :], ln_b[i, 0][None, :],
                            (vec(i, 3), vec(i, 4)), alpha=alpha, n_lat_tiles=n_lat_tiles, n_tiles=n_out_tiles,
                            name="mixer_out_ln")
        rows_out = n_out_tiles * ROW_TILE
        act = _ffn_in(h.reshape(b * rows_out, d), ffn_w_in, i, MM_ROW_TILE, _col_tile(ffn_w_in.shape[2] // 2))
        act = act.reshape(b, rows_out, -1)
        if last:
            xs, = _outproj_ln(act, ffn_w_out[i].astype(BF16), xs, vec(i, 5), ln_g[i, 1][None, :], ln_b[i, 1][None, :],
                              (), alpha=alpha, n_lat_tiles=n_lat_tiles, n_tiles=n_lat_tiles, name="ffn_out_ln_final")
        else:
            xs, h = _outproj_ln(act, ffn_w_out[i].astype(BF16), xs, vec(i, 5), ln_g[i, 1][None, :], ln_b[i, 1][None, :],
                                (vec(i + 1, 0), vec(i + 1, 1)), alpha=alpha, n_lat_tiles=n_lat_tiles,
                                n_tiles=n_tiles, name="ffn_out_ln")
    return xs
```
